```python
import jax, jax.numpy as jnp
from jax import lax
import numpy as np


D_MODEL = 1024
BATCH = 16
SEQ = 4096
DEPTH = 1

PLE_DIM = 256
M_HEADS = 4
M_HEAD_DIM = 128
M_WIDTH = M_HEADS * M_HEAD_DIM
M_CHUNK = 64
A_HEADS = 8
A_KV_GROUPS = 2
A_HEAD_DIM = 64
A_WIDTH = A_HEADS * A_HEAD_DIM
KV_WIDTH = A_KV_GROUPS * A_HEAD_DIM
CMP_BLOCK = 32
CMP_STRIDE = 16
CMP_HIDDEN = 128
SEL_BLOCK = 64
N_SELECT = 16
WINDOW = 512
A_Q_BLOCK = 32
ALIBI_MAX = 8.0
MIX_WIDTH = M_WIDTH + A_WIDTH
IN_WIDTH = 4 * M_WIDTH + 2 * M_HEADS + A_WIDTH + 6 * KV_WIDTH + 3 * A_HEADS
D_FF = 2048
CONV_WIDTH = 3
EPS = 1e-6
NEG = -1e30
BIG = 1e30

kernel_name = "hymba_mlstm_nsa_convffn_ple"


def rms_norm(x, g):
    xf = x.astype(jnp.float32)
    y = xf * lax.rsqrt(jnp.mean(xf * xf, axis=-1, keepdims=True) + EPS)
    return (y * g.astype(jnp.float32)).astype(x.dtype)


def alibi_slopes(n):
    return jnp.exp2(-ALIBI_MAX * jnp.arange(1, n + 1, dtype=jnp.float32) / n)


def mlstm(q, k, v, i_pre, f_pre):
    B, S, H, dh = q.shape
    L = M_CHUNK
    NC = S // L
    f32 = jnp.float32

    def chunks(t):
        t = t.astype(f32).reshape((B, NC, L, H) + t.shape[3:])
        return jnp.moveaxis(t, 3, 1)

    qc = chunks(q) * (dh ** -0.5)
    kc = chunks(k)
    vc = chunks(v)
    ig = chunks(i_pre)
    b = jnp.cumsum(jax.nn.log_sigmoid(chunks(f_pre)), axis=-1)
    bL = b[..., -1]
    a = bL[..., None] - b + ig

    def step(carry, inp):
        C, n, m = carry
        k_, v_, a_, bL_ = inp
        m_new = jnp.maximum(bL_ + m, jnp.max(a_, axis=-1))
        decay = jnp.exp(bL_ + m - m_new)
        w = jnp.exp(a_ - m_new[..., None])
        C_new = decay[..., None, None] * C + jnp.einsum('bhl,bhlk,bhlv->bhkv', w, k_, v_)
        n_new = decay[..., None] * n + jnp.einsum('bhl,bhlk->bhk', w, k_)
        return (C_new, n_new, m_new), (C, n, m)

    init = (jnp.zeros((B, H, dh, dh), f32), jnp.zeros((B, H, dh), f32), jnp.zeros((B, H), f32))
    xs = (jnp.moveaxis(kc, 2, 0), jnp.moveaxis(vc, 2, 0), jnp.moveaxis(a, 2, 0), jnp.moveaxis(bL, 2, 0))
    _, (Cs, ns, ms) = lax.scan(step, init, xs)
    Cs = jnp.moveaxis(Cs, 0, 2)
    ns = jnp.moveaxis(ns, 0, 2)
    ms = jnp.moveaxis(ms, 0, 2)

    bq = b + ms[..., None]
    causal = jnp.tril(jnp.ones((L, L), dtype=bool))
    Dm = jnp.where(causal, b[..., :, None] - b[..., None, :] + ig[..., None, :], -jnp.inf)
    m = jnp.maximum(bq, jnp.max(Dm, axis=-1))
    inter = jnp.exp(bq - m)
    Wt = jnp.exp(Dm - m[..., None]) * jnp.einsum('bhcld,bhcsd->bhcls', qc, kc)
    num = inter[..., None] * jnp.einsum('bhcld,bhcdv->bhclv', qc, Cs) + jnp.einsum('bhcls,bhcsv->bhclv', Wt, vc)
    den = inter * jnp.einsum('bhcld,bhcd->bhcl', qc, ns) + jnp.sum(Wt, axis=-1)
    h = num / jnp.maximum(jnp.abs(den), jnp.exp(-m))[..., None]
    return jnp.moveaxis(h, 1, 3).reshape(B, S, H, dh)


def nsa(q, kc_raw, vc_raw, ks_raw, vs_raw, kw_raw, vw_raw, g_pre, pe_k, pe_v, ck_w1, ck_w2, cv_w1, cv_w2):
    f32 = jnp.float32
    B, S, H, dh = q.shape
    G = A_KV_GROUPS
    R = H // G
    qg = q.astype(f32).reshape(B, S, G, R, dh).transpose(0, 2, 3, 1, 4) * (dh ** -0.5)
    gates = jax.nn.sigmoid(g_pre.astype(f32).reshape(B, S, G, R, 3)).transpose(0, 2, 3, 1, 4)
    to_g = lambda t: t.astype(f32).transpose(0, 2, 1, 3)

    Nc = (S - CMP_BLOCK) // CMP_STRIDE + 1
    cidx = jnp.arange(Nc)[:, None] * CMP_STRIDE + jnp.arange(CMP_BLOCK)[None, :]

    def compress(t, pe, w1, w2):
        blk = to_g(t)[:, :, cidx] + pe.astype(f32)
        flat = blk.reshape(B, G, Nc, CMP_BLOCK * dh)
        return jax.nn.gelu(flat @ w1.astype(f32)) @ w2.astype(f32)

    Kc = compress(kc_raw, pe_k, ck_w1, ck_w2)
    Vc = compress(vc_raw, pe_v, cv_w1, cv_w2)
    cmp_start = jnp.arange(Nc) * CMP_STRIDE
    cmp_end = cmp_start + CMP_BLOCK - 1

    Ns = S // SEL_BLOCK
    n_sel = min(N_SELECT, Ns)
    Ks = to_g(ks_raw).reshape(B, G, Ns, SEL_BLOCK, dh)
    Vs = to_g(vs_raw).reshape(B, G, Ns, SEL_BLOCK, dh)
    sel_start = jnp.arange(Ns) * SEL_BLOCK
    overlap = ((cmp_start[:, None] <= sel_start[None, :] + SEL_BLOCK - 1)
               & (cmp_end[:, None] >= sel_start[None, :])).astype(f32)

    pad = ((0, 0), (0, 0), (WINDOW, 0), (0, 0))
    Kw = jnp.pad(to_g(kw_raw), pad)
    Vw = jnp.pad(to_g(vw_raw), pad)

    slope = alibi_slopes(H).reshape(1, G, R, 1, 1)
    bi = jnp.arange(B)[:, None, None, None]
    gi = jnp.arange(G)[None, :, None, None]
    T = A_Q_BLOCK

    def block(qb):
        t0 = qb * T
        t = t0 + jnp.arange(T)
        qblk = lax.dynamic_slice_in_dim(qg, t0, T, axis=3)
        gblk = lax.dynamic_slice_in_dim(gates, t0, T, axis=3)

        valid_c = cmp_end[None, :] <= t[:, None]
        dist_c = (t[:, None] - cmp_end[None, :]).astype(f32)
        s_c = jnp.einsum('bgrtd,bgnd->bgrtn', qblk, Kc) - slope * dist_c
        p_c = jnp.where(valid_c, jax.nn.softmax(jnp.where(valid_c, s_c, NEG), axis=-1), 0.0)
        o_c = jnp.einsum('bgrtn,bgnd->bgrtd', p_c, Vc)

        imp = jnp.einsum('bgrtn,nj->bgtj', p_c, overlap)
        j = jnp.arange(Ns)[None, :]
        cur = (t // SEL_BLOCK)[:, None]
        forced = (j == 0) | (j == cur) | (j == cur - 1)
        causal_b = sel_start[None, :] <= t[:, None]
        score = jnp.where(causal_b, jnp.where(forced, BIG, imp), NEG)
        sel_idx = lax.top_k(score, n_sel)[1]
        Ksel = Ks[bi, gi, sel_idx]
        Vsel = Vs[bi, gi, sel_idx]
        pos_s = sel_idx[..., None] * SEL_BLOCK + jnp.arange(SEL_BLOCK)
        dist_s = (t[None, None, :, None, None] - pos_s)
        valid_s = (dist_s >= 0)[:, :, None]
        s_s = jnp.einsum('bgrtd,bgtnld->bgrtnl', qblk, Ksel) - slope[..., None] * dist_s[:, :, None].astype(f32)
        s_s = jnp.where(valid_s, s_s, NEG).reshape(B, G, R, T, n_sel * SEL_BLOCK)
        p_s = jax.nn.softmax(s_s, axis=-1).reshape(B, G, R, T, n_sel, SEL_BLOCK)
        o_s = jnp.einsum('bgrtnl,bgtnld->bgrtd', p_s, Vsel)

        kwin = lax.dynamic_slice_in_dim(Kw, t0, WINDOW + T, axis=2)
        vwin = lax.dynamic_slice_in_dim(Vw, t0, WINDOW + T, axis=2)
        pos_w = t0 - WINDOW + jnp.arange(WINDOW + T)
        dist_w = t[:, None] - pos_w[None, :]
        valid_w = (pos_w[None, :] >= 0) & (dist_w >= 0) & (dist_w < WINDOW)
        s_w = jnp.einsum('bgrtd,bgsd->bgrts', qblk, kwin) - slope * dist_w.astype(f32)
        p_w = jax.nn.softmax(jnp.where(valid_w, s_w, NEG), axis=-1)
        o_w = jnp.einsum('bgrts,bgsd->bgrtd', p_w, vwin)

        return gblk[..., 0:1] * o_c + gblk[..., 1:2] * o_s + gblk[..., 2:3] * o_w

    outs = lax.map(block, jnp.arange(S // T))
    return outs.transpose(1, 0, 4, 2, 3, 5).reshape(B, S, H * dh)


def conv_ffn(h, w_up, conv_w, conv_b, w_down):
    u = h @ w_up
    gate, val = jnp.split(u, 2, axis=-1)
    gate = lax.conv_general_dilated(gate, conv_w[:, None, :].astype(gate.dtype), window_strides=(1,),
                                    padding=[(CONV_WIDTH - 1, 0)],
                                    dimension_numbers=('NWC', 'WIO', 'NWC'),
                                    feature_group_count=D_FF) + conv_b
    return (jax.nn.silu(gate) * val) @ w_down


def setup_inputs(seed: int = 0) -> dict:
    key = jax.random.key(seed)
    ks = jax.random.split(key, 24)
    f32 = jnp.float32
    nrm = lambda k, shape, scale: jax.random.normal(k, shape, f32) * scale
    gain = lambda k, shape: 1.0 + 0.05 * jax.random.normal(k, shape, f32)
    i_bias = nrm(ks[4], (DEPTH, M_HEADS), 0.1)
    f_bias = jnp.linspace(3.0, 6.0, M_HEADS, dtype=f32)[None, :] + nrm(ks[5], (DEPTH, M_HEADS), 0.1)
    return {
        "x": nrm(ks[0], (BATCH, SEQ, D_MODEL), 1.0),
        "p": nrm(ks[1], (DEPTH, BATCH, SEQ, PLE_DIM), 1.0),
        "ln1_g": gain(ks[2], (DEPTH, D_MODEL)),
        "w_in": nrm(ks[3], (DEPTH, D_MODEL, IN_WIDTH), D_MODEL ** -0.5),
        "mlstm_gate_bias": jnp.concatenate([i_bias, f_bias], axis=-1),
        "mlstm_norm_g": gain(ks[6], (DEPTH, M_WIDTH)),
        "cmp_pos_k": nrm(ks[7], (DEPTH, CMP_BLOCK, A_HEAD_DIM), 0.02),
        "cmp_pos_v": nrm(ks[8], (DEPTH, CMP_BLOCK, A_HEAD_DIM), 0.02),
        "cmp_k_w1": nrm(ks[9], (DEPTH, CMP_BLOCK * A_HEAD_DIM, CMP_HIDDEN), (CMP_BLOCK * A_HEAD_DIM) ** -0.5),
        "cmp_k_w2": nrm(ks[10], (DEPTH, CMP_HIDDEN, A_HEAD_DIM), CMP_HIDDEN ** -0.5),
        "cmp_v_w1": nrm(ks[11], (DEPTH, CMP_BLOCK * A_HEAD_DIM, CMP_HIDDEN), (CMP_BLOCK * A_HEAD_DIM) ** -0.5),
        "cmp_v_w2": nrm(ks[12], (DEPTH, CMP_HIDDEN, A_HEAD_DIM), CMP_HIDDEN ** -0.5),
        "w_out": nrm(ks[13], (DEPTH, MIX_WIDTH, D_MODEL), MIX_WIDTH ** -0.5),
        "ln2_g": gain(ks[14], (DEPTH, D_MODEL)),
        "w_up": nrm(ks[15], (DEPTH, D_MODEL, 2 * D_FF), D_MODEL ** -0.5),
        "conv_w": nrm(ks[16], (DEPTH, CONV_WIDTH, D_FF), CONV_WIDTH ** -0.5),
        "conv_b": nrm(ks[17], (DEPTH, D_FF), 0.02),
        "w_down": nrm(ks[18], (DEPTH, D_FF, D_MODEL), D_FF ** -0.5),
        "ple_norm_g": gain(ks[19], (DEPTH, D_MODEL)),
        "w_ple_gate": nrm(ks[20], (DEPTH, D_MODEL, D_MODEL), D_MODEL ** -0.5),
        "w_ple_proj": nrm(ks[21], (DEPTH, PLE_DIM, D_MODEL), PLE_DIM ** -0.5),
        "final_g": gain(ks[22], (D_MODEL,)),
    }


def reference(x, p, ln1_g, w_in, mlstm_gate_bias, mlstm_norm_g, cmp_pos_k, cmp_pos_v, cmp_k_w1, cmp_k_w2,
              cmp_v_w1, cmp_v_w2, w_out, ln2_g, w_up, conv_w, conv_b, w_down, ple_norm_g, w_ple_gate,
              w_ple_proj, final_g):
    B, S, _ = x.shape
    sizes = [M_WIDTH] * 4 + [M_HEADS, M_HEADS, A_WIDTH] + [KV_WIDTH] * 6 + [3 * A_HEADS]
    offs = np.cumsum(sizes)[:-1].tolist()
    for i in range(DEPTH):
        h = rms_norm(x, ln1_g[i])
        z = h @ w_in[i]
        mq, mk, mv, mo, mi, mf, aq, akc, avc, aks, avs, akw, avw, ag = jnp.split(z, offs, axis=-1)
        mshape = (B, S, M_HEADS, M_HEAD_DIM)
        hm = mlstm(mq.reshape(mshape), mk.reshape(mshape), mv.reshape(mshape),
                   mi + mlstm_gate_bias[i, :M_HEADS], mf + mlstm_gate_bias[i, M_HEADS:])
        hm = rms_norm(hm, mlstm_norm_g[i].reshape(M_HEADS, M_HEAD_DIM)).reshape(B, S, M_WIDTH)
        m_out = (hm * jax.nn.sigmoid(mo.astype(jnp.float32))).astype(x.dtype)
        kvshape = (B, S, A_KV_GROUPS, A_HEAD_DIM)
        a_out = nsa(aq.reshape(B, S, A_HEADS, A_HEAD_DIM), akc.reshape(kvshape), avc.reshape(kvshape),
                    aks.reshape(kvshape), avs.reshape(kvshape), akw.reshape(kvshape), avw.reshape(kvshape),
                    ag, cmp_pos_k[i], cmp_pos_v[i], cmp_k_w1[i], cmp_k_w2[i], cmp_v_w1[i], cmp_v_w2[i]).astype(x.dtype)
        x = x + jnp.concatenate([m_out, a_out], axis=-1) @ w_out[i]
        x = x + conv_ffn(rms_norm(x, ln2_g[i]), w_up[i], conv_w[i], conv_b[i], w_down[i])
        gate = jax.nn.sigmoid(rms_norm(x, ple_norm_g[i]) @ w_ple_gate[i])
        x = x + gate * (p[i].astype(x.dtype) @ w_ple_proj[i])
    return rms_norm(x, final_g)
```

```python
import functools

import jax
import jax.numpy as jnp
from jax import lax
from jax.experimental import pallas as pl
from jax.experimental.pallas import tpu as pltpu

F32 = jnp.float32
BF16 = jnp.bfloat16

M_HEADS = 4
M_HEAD_DIM = 128
M_WIDTH = M_HEADS * M_HEAD_DIM
A_HEADS = 8
A_KV_GROUPS = 2
A_REP = A_HEADS // A_KV_GROUPS
A_HEAD_DIM = 64
A_WIDTH = A_HEADS * A_HEAD_DIM
KV_WIDTH = A_KV_GROUPS * A_HEAD_DIM
CMP_BLOCK = 32
CMP_STRIDE = 16
CMP_HIDDEN = 128
SEL_BLOCK = 64
N_SELECT = 16
WINDOW = 512
ALIBI_MAX = 8.0
D_FF = 2048
EPS = 1e-6
NEG = -1e30
BIG = 1e30

LANES = 128
VMEM_LIMIT = 56 * 1024 * 1024

NB = 3 * M_WIDTH + A_WIDTH + 6 * KV_WIDTH
NO = M_WIDTH
NSM = LANES
AQ_OFF = 3 * M_WIDTH
KV_OFF = AQ_OFF + A_WIDTH
AG_OFF = 2 * M_HEADS


def _dot(a, b):
    return jnp.dot(a, b, preferred_element_type=F32)


def _dot_nt(a, b):
    return lax.dot_general(a, b, (((1,), (1,)), ((), ())), preferred_element_type=F32)


def _rms(x, g):
    return x * lax.rsqrt(jnp.mean(x * x, axis=-1, keepdims=True) + EPS) * g


def _log_sigmoid(x):
    return jnp.minimum(x, 0.0) - jnp.log1p(jnp.exp(-jnp.abs(x)))


def _resident(shape):
    zeros = (0,) * len(shape)
    return pl.BlockSpec(shape, lambda *_: zeros, pipeline_mode=pl.Buffered(1))


def _inproj_body(x_ref, g_ref, w_ref, bias_ref, zb_ref, zo_ref, sm_ref, smt_ref):
    h = _rms(x_ref[...], g_ref[...]).astype(BF16)
    cw = 512
    for c0 in range(0, NB, cw):
        c1 = min(c0 + cw, NB)
        zb_ref[:, c0:c1] = _dot(h, w_ref[:, c0:c1]).astype(BF16)
    zo_ref[...] = _dot(h, w_ref[:, NB:NB + NO])
    sm = _dot(h, w_ref[:, NB + NO:]) + bias_ref[...]
    sm_ref[...] = sm
    smt_ref[...] = sm.T[:2 * M_HEADS]


def _inproj(x2d, g, w, bias, tm):
    n, d = x2d.shape
    return pl.pallas_call(
        _inproj_body,
        grid=(n // tm,),
        in_specs=[
            pl.BlockSpec((tm, d), lambda i: (i, 0)),
            _resident((1, d)),
            _resident(w.shape),
            _resident((1, NSM)),
        ],
        out_specs=[
            pl.BlockSpec((tm, NB), lambda i: (i, 0)),
            pl.BlockSpec((tm, NO), lambda i: (i, 0)),
            pl.BlockSpec((tm, NSM), lambda i: (i, 0)),
            pl.BlockSpec((2 * M_HEADS, tm), lambda i: (0, i)),
        ],
        out_shape=[
            jax.ShapeDtypeStruct((n, NB), BF16),
            jax.ShapeDtypeStruct((n, NO), F32),
            jax.ShapeDtypeStruct((n, NSM), F32),
            jax.ShapeDtypeStruct((2 * M_HEADS, n), F32),
        ],
        compiler_params=pltpu.CompilerParams(
            dimension_semantics=("arbitrary",), vmem_limit_bytes=VMEM_LIMIT),
    )(x2d, g, w, bias)


def _mlstm_body(zb_ref, zo_ref, sm_ref, smt_ref, gn_ref, out_ref, c_ref, n_ref, m_ref, *, L):
    @pl.when(pl.program_id(1) == 0)
    def _():
        c_ref[...] = jnp.zeros_like(c_ref)
        n_ref[...] = jnp.zeros_like(n_ref)
        m_ref[...] = jnp.zeros_like(m_ref)

    dh = M_HEAD_DIM
    scale = dh ** -0.5
    li = lax.broadcasted_iota(jnp.int32, (L, L), 0)
    si = lax.broadcasted_iota(jnp.int32, (L, L), 1)
    causal = si <= li
    sm = sm_ref[...]
    smt = smt_ref[...]
    for h in range(M_HEADS):
        q = zb_ref[:, h * dh:(h + 1) * dh]
        k = zb_ref[:, M_WIDTH + h * dh:M_WIDTH + (h + 1) * dh]
        v = zb_ref[:, 2 * M_WIDTH + h * dh:2 * M_WIDTH + (h + 1) * dh]
        ig_col = sm[:, h:h + 1]
        ig_row = smt[h:h + 1, :]
        lf_col = _log_sigmoid(sm[:, M_HEADS + h:M_HEADS + h + 1])
        lf_row = _log_sigmoid(smt[M_HEADS + h:M_HEADS + h + 1, :])
        b_col = jnp.sum(jnp.where(causal, lf_row, 0.0), axis=1, keepdims=True)
        b_row = jnp.sum(jnp.where(li <= si, lf_col, 0.0), axis=0, keepdims=True)
        b_last = jnp.sum(lf_row, axis=1, keepdims=True)
        m_prev = m_ref[h][:, :1]

        dm = jnp.where(causal, b_col - b_row + ig_row, -jnp.inf)
        bq = b_col + m_prev
        m_loc = jnp.maximum(bq, jnp.max(dm, axis=1, keepdims=True))
        inter = jnp.exp(bq - m_loc)
        wt = jnp.exp(dm - m_loc) * _dot_nt(q, k)
        c_prev = c_ref[h]
        n_prev = n_ref[h]
        num = inter * _dot(q, c_prev.astype(BF16)) + _dot(wt.astype(BF16), v)
        den = (inter * jnp.sum(q.astype(F32) * n_prev, axis=1, keepdims=True)
               + jnp.sum(wt, axis=1, keepdims=True))
        hh = (num * scale) / jnp.maximum(jnp.abs(den) * scale, jnp.exp(-m_loc))
        hn = _rms(hh, gn_ref[:, h * dh:(h + 1) * dh])
        o = zo_ref[:, h * dh:(h + 1) * dh]
        out_ref[:, h * dh:(h + 1) * dh] = (hn * jax.nn.sigmoid(o)).astype(out_ref.dtype)

        a_col = b_last - b_col + ig_col
        a_row = b_last - b_row + ig_row
        m_new = jnp.maximum(b_last + m_prev, jnp.max(a_row, axis=1, keepdims=True))
        decay = jnp.exp(b_last + m_prev - m_new)
        kf = k.astype(F32)
        kw_t = (kf.T * jnp.exp(a_row - m_new)).astype(BF16)
        c_ref[h] = decay * c_prev + _dot(kw_t, v)
        n_ref[h] = decay * n_prev + jnp.sum(kf * jnp.exp(a_col - m_new), axis=0, keepdims=True)
        m_ref[h] = jnp.broadcast_to(m_new, (1, LANES))


def _mlstm(zb, zo, sm, smt, gn, batch, seq, L):
    n = batch * seq
    nc = seq // L
    return pl.pallas_call(
        functools.partial(_mlstm_body, L=L),
        grid=(batch, nc),
        in_specs=[
            pl.BlockSpec((L, 3 * M_WIDTH), lambda b, c: (b * nc + c, 0)),
            pl.BlockSpec((L, NO), lambda b, c: (b * nc + c, 0)),
            pl.BlockSpec((L, NSM), lambda b, c: (b * nc + c, 0)),
            pl.BlockSpec((2 * M_HEADS, L), lambda b, c: (0, b * nc + c)),
            _resident((1, M_WIDTH)),
        ],
        out_specs=pl.BlockSpec((L, M_WIDTH), lambda b, c: (b * nc + c, 0)),
        out_shape=jax.ShapeDtypeStruct((n, M_WIDTH), BF16),
        scratch_shapes=[
            pltpu.VMEM((M_HEADS, M_HEAD_DIM, M_HEAD_DIM), F32),
            pltpu.VMEM((M_HEADS, 1, M_HEAD_DIM), F32),
            pltpu.VMEM((M_HEADS, 1, LANES), F32),
        ],
        compiler_params=pltpu.CompilerParams(
            dimension_semantics=("arbitrary", "arbitrary"), vmem_limit_bytes=VMEM_LIMIT),
    )(zb, zo, sm, smt, gn)


def _compress_body(x_ref, pe_ref, w1_ref, w2_ref, out_ref, *, ncp):
    half = (CMP_BLOCK // 2) * A_HEAD_DIM
    x = x_ref[...]
    first = _dot(x, w1_ref[:half, :])
    second = _dot(x, w1_ref[half:, :])
    second = pltpu.roll(second, ncp - 1, 0)
    pe_term = _dot(pe_ref[...], w1_ref[...])[0:1]
    hid = jax.nn.gelu(first + second + pe_term)
    out_ref[...] = _dot(hid.astype(BF16), w2_ref[...]).astype(out_ref.dtype)


def _compress(x2, pe, w1, w2):
    two, bg, ncp, width = x2.shape
    return pl.pallas_call(
        functools.partial(_compress_body, ncp=ncp),
        grid=(two, bg),
        in_specs=[
            pl.BlockSpec((None, None, ncp, width), lambda a, i: (a, i, 0, 0)),
            pl.BlockSpec((None, 8, CMP_BLOCK * A_HEAD_DIM), lambda a, i: (a, 0, 0)),
            pl.BlockSpec((None, CMP_BLOCK * A_HEAD_DIM, CMP_HIDDEN), lambda a, i: (a, 0, 0)),
            pl.BlockSpec((None, CMP_HIDDEN, A_HEAD_DIM), lambda a, i: (a, 0, 0)),
        ],
        out_specs=pl.BlockSpec((None, None, ncp, A_HEAD_DIM), lambda a, i: (a, i, 0, 0)),
        out_shape=jax.ShapeDtypeStruct((two, bg, ncp, A_HEAD_DIM), BF16),
        compiler_params=pltpu.CompilerParams(
            dimension_semantics=("arbitrary", "arbitrary"), vmem_limit_bytes=VMEM_LIMIT),
    )(x2, pe, w1, w2)


def _nsa_body(q_ref, sm_ref, kc_ref, vc_ref, ks_ref, vs_ref, kw_ref, vw_ref, e_ref, out_ref, mb_ref,
              *, TQ, TK, S, NCP):
    g = pl.program_id(1)
    t0 = pl.program_id(2) * TQ
    R = A_REP
    M = R * TQ
    NS = S // SEL_BLOCK
    n_sel = min(N_SELECT, NS)
    scale = A_HEAD_DIM ** -0.5
    q = q_ref[...].reshape(M, A_HEAD_DIM)

    row = lax.broadcasted_iota(jnp.int32, (M, 1), 0)
    tcol = t0 + (row & (TQ - 1))
    head = row >> (TQ.bit_length() - 1)
    slope = jnp.zeros((M, 1), F32)
    for r in range(R):
        s_r = jnp.where(g == 0, 2.0 ** (-ALIBI_MAX * (r + 1) / A_HEADS),
                        2.0 ** (-ALIBI_MAX * (R + r + 1) / A_HEADS)).astype(F32)
        slope = jnp.where(head == r, s_r, slope)

    n_i = lax.broadcasted_iota(jnp.int32, (1, NCP), 1)
    cend = n_i * CMP_STRIDE + (CMP_BLOCK - 1)
    valid_c = cend <= tcol
    s = _dot_nt(q, kc_ref[...]) * scale - slope * (tcol - cend).astype(F32)
    s = jnp.where(valid_c, s, NEG)
    e = jnp.where(valid_c, jnp.exp(s - jnp.max(s, axis=1, keepdims=True)), 0.0)
    p_c = e / jnp.maximum(jnp.sum(e, axis=1, keepdims=True), 1e-30)
    o_c = _dot(p_c.astype(BF16), vc_ref[...])

    p_sum = p_c[0:TQ]
    for r in range(1, R):
        p_sum = p_sum + p_c[r * TQ:(r + 1) * TQ]
    jc = lax.broadcasted_iota(jnp.int32, (NS, 1), 0)
    nr = lax.broadcasted_iota(jnp.int32, (1, NCP), 1)
    overlap_t = ((nr * CMP_STRIDE <= jc * SEL_BLOCK + (SEL_BLOCK - 1))
                 & (nr * CMP_STRIDE + (CMP_BLOCK - 1) >= jc * SEL_BLOCK))
    imp_t = _dot_nt(jnp.where(overlap_t, 1.0, 0.0).astype(BF16), p_sum.astype(BF16))
    cur = (t0 + lax.broadcasted_iota(jnp.int32, (1, TQ), 1)) >> (SEL_BLOCK.bit_length() - 1)
    forced = (jc == 0) | (jc == cur) | (jc == cur - 1)
    score = jnp.where(jc <= cur, jnp.where(forced, BIG, imp_t), NEG)
    rank = jnp.zeros((NS, TQ), jnp.int32)
    for i in range(NS):
        s_i = score[i:i + 1, :]
        beats = (s_i > score) | ((s_i == score) & (jc > i))
        rank = rank + jnp.where(beats, 1, 0)
    sel_bias = jnp.where(rank < n_sel, 0.0, NEG).T.astype(BF16)
    for kt in range(S // TK):
        mb_ref[kt] = _dot(sel_bias, e_ref[:, kt * TK:(kt + 1) * TK])

    def sel_step(kt, carry):
        m_i, l_i, acc = carry
        k0 = pl.multiple_of(kt * TK, TK)
        k = ks_ref[pl.ds(k0, TK), :]
        v = vs_ref[pl.ds(k0, TK), :]
        dist = tcol - (k0 + lax.broadcasted_iota(jnp.int32, (1, TK), 1))
        s = _dot_nt(q, k) * scale - slope * dist.astype(F32)
        s = (s.reshape(R, TQ, TK) + mb_ref[kt][None]).reshape(M, TK)
        s = jnp.where(dist >= 0, s, NEG)
        m_new = jnp.maximum(m_i, jnp.max(s, axis=1, keepdims=True))
        alpha = jnp.exp(m_i - m_new)
        p = jnp.exp(s - m_new)
        l_new = alpha * l_i + jnp.sum(p, axis=1, keepdims=True)
        acc = alpha * acc + _dot(p.astype(BF16), v)
        return m_new, l_new, acc

    init = (jnp.full((M, 1), NEG, F32), jnp.zeros((M, 1), F32), jnp.zeros((M, A_HEAD_DIM), F32))
    _, l_s, acc_s = lax.fori_loop(0, (t0 + TQ + TK - 1) // TK, sel_step, init)
    o_s = acc_s / l_s

    wk = min(WINDOW + TQ, S)
    k0 = pl.multiple_of(jnp.maximum(t0 + TQ - wk, 0), TQ)
    k = kw_ref[pl.ds(k0, wk), :]
    v = vw_ref[pl.ds(k0, wk), :]
    dist = tcol - (k0 + lax.broadcasted_iota(jnp.int32, (1, wk), 1))
    s = _dot_nt(q, k) * scale - slope * dist.astype(F32)
    s = jnp.where((dist >= 0) & (dist < WINDOW), s, NEG)
    p = jnp.exp(s - jnp.max(s, axis=1, keepdims=True))
    o_w = _dot(p.astype(BF16), v) / jnp.sum(p, axis=1, keepdims=True)

    sm = sm_ref[...]
    for r in range(R):
        rows = slice(r * TQ, (r + 1) * TQ)
        gates = []
        for c in range(3):
            c0 = AG_OFF + r * 3 + c
            c1 = AG_OFF + (R + r) * 3 + c
            gates.append(jax.nn.sigmoid(jnp.where(g == 0, sm[:, c0:c0 + 1], sm[:, c1:c1 + 1])))
        o = gates[0] * o_c[rows] + gates[1] * o_s[rows] + gates[2] * o_w[rows]
        out_ref[:, r * A_HEAD_DIM:(r + 1) * A_HEAD_DIM] = o.astype(out_ref.dtype)


def _nsa(q4, sm, kc, vc, ks, vs, kw, vw, expand, batch, seq, TQ, TK):
    n = batch * seq
    G = A_KV_GROUPS
    nq = seq // TQ
    ncp = kc.shape[1]
    kv_spec = pl.BlockSpec((None, seq, A_HEAD_DIM), lambda b, g, i: (b * G + g, 0, 0))
    kc_spec = pl.BlockSpec((None, ncp, A_HEAD_DIM), lambda b, g, i: (b * G + g, 0, 0))
    return pl.pallas_call(
        functools.partial(_nsa_body, TQ=TQ, TK=TK, S=seq, NCP=ncp),
        grid=(batch, G, nq),
        in_specs=[
            pl.BlockSpec((None, A_REP, TQ, A_HEAD_DIM), lambda b, g, i: (b, g, i, 0)),
            pl.BlockSpec((TQ, NSM), lambda b, g, i: (b * nq + i, 0)),
            kc_spec, kc_spec, kv_spec, kv_spec, kv_spec, kv_spec,
            _resident(expand.shape),
        ],
        out_specs=pl.BlockSpec((TQ, A_REP * A_HEAD_DIM), lambda b, g, i: (b * nq + i, g)),
        out_shape=jax.ShapeDtypeStruct((n, A_WIDTH), BF16),
        scratch_shapes=[pltpu.VMEM((seq // TK, TQ, TK), F32)],
        compiler_params=pltpu.CompilerParams(
            dimension_semantics=("arbitrary", "arbitrary", "arbitrary"), vmem_limit_bytes=VMEM_LIMIT),
    )(q4, sm, kc, vc, ks, vs, kw, vw, expand)


def _outblock_body(x_ref, mo_ref, ao_ref, p_ref, wout_ref, ln2_ref, wup_ref, cw_ref, cb_ref, wdn_ref,
                   pg_ref, wpg_ref, wpp_ref, fg_ref, out_ref, tail_ref, *, tm, tiles_per_seq, fc):
    @pl.when(pl.program_id(0) % tiles_per_seq == 0)
    def _():
        tail_ref[...] = jnp.zeros_like(tail_ref)

    x1 = (x_ref[...] + _dot(mo_ref[...], wout_ref[:M_WIDTH, :]) + _dot(ao_ref[...], wout_ref[M_WIDTH:, :]))
    h2 = _rms(x1, ln2_ref[...]).astype(BF16)
    rowi = lax.broadcasted_iota(jnp.int32, (tm, 1), 0)
    acc = jnp.zeros(x1.shape, F32)
    for c0 in range(0, D_FF, fc):
        gate = _dot(h2, wup_ref[:, c0:c0 + fc])
        val = _dot(h2, wup_ref[:, D_FF + c0:D_FF + c0 + fc])
        tail = tail_ref[:, c0:c0 + fc]
        g_m1 = jnp.where(rowi == 0, tail[7:8], pltpu.roll(gate, 1, 0))
        g_m2 = jnp.where(rowi == 0, tail[6:7], jnp.where(rowi == 1, tail[7:8], pltpu.roll(gate, 2, 0)))
        conv = (cw_ref[0:1, c0:c0 + fc] * g_m2 + cw_ref[1:2, c0:c0 + fc] * g_m1
                + cw_ref[2:3, c0:c0 + fc] * gate + cb_ref[:, c0:c0 + fc])
        y = jax.nn.silu(conv) * val
        acc = acc + _dot(y.astype(BF16), wdn_ref[c0:c0 + fc, :])
        tail_ref[:, c0:c0 + fc] = gate[tm - 8:tm]
    x2 = x1 + acc
    gate2 = jax.nn.sigmoid(_dot(_rms(x2, pg_ref[...]).astype(BF16), wpg_ref[...]))
    x3 = x2 + gate2 * _dot(p_ref[...].astype(BF16), wpp_ref[...])
    out_ref[...] = _rms(x3, fg_ref[...])


def _outblock(x2d, mo, ao, p2d, wout, ln2, wup, cw, cb, wdn, pg, wpg, wpp, fg, seq, tm):
    n, d = x2d.shape
    row = lambda width: pl.BlockSpec((tm, width), lambda i: (i, 0))
    return pl.pallas_call(
        functools.partial(_outblock_body, tm=tm, tiles_per_seq=seq // tm, fc=512),
        grid=(n // tm,),
        in_specs=[
            row(d), row(M_WIDTH), row(A_WIDTH), row(p2d.shape[1]),
            _resident(wout.shape), _resident((1, d)), _resident(wup.shape), _resident(cw.shape),
            _resident((1, D_FF)), _resident(wdn.shape), _resident((1, d)), _resident(wpg.shape),
            _resident(wpp.shape), _resident((1, d)),
        ],
        out_specs=row(d),
        out_shape=jax.ShapeDtypeStruct((n, d), F32),
        scratch_shapes=[pltpu.VMEM((8, D_FF), F32)],
        compiler_params=pltpu.CompilerParams(
            dimension_semantics=("arbitrary",), vmem_limit_bytes=VMEM_LIMIT),
    )(x2d, mo, ao, p2d, wout, ln2, wup, cw, cb, wdn, pg, wpg, wpp, fg)


def _layer(x2d, p2d, batch, seq, ln1_g, w_in, gate_bias, mnorm_g, pe_k, pe_v, ck_w1, ck_w2, cv_w1, cv_w2,
           w_out, ln2_g, w_up, conv_w, conv_b, w_down, ple_g, w_pg, w_pp, out_g):
    d = x2d.shape[1]
    G = A_KV_GROUPS
    dh = A_HEAD_DIM
    o_mo = 3 * M_WIDTH
    o_mi = o_mo + M_WIDTH
    o_aq = o_mi + 2 * M_HEADS
    o_ag = o_aq + A_WIDTH + 6 * KV_WIDTH
    w = jnp.concatenate([
        w_in[:, :o_mo], w_in[:, o_aq:o_ag],
        w_in[:, o_mo:o_mi],
        w_in[:, o_mi:o_aq], w_in[:, o_ag:], jnp.zeros((d, NSM - 2 * M_HEADS - 3 * A_HEADS), w_in.dtype),
    ], axis=1).astype(BF16)
    bias = jnp.concatenate([gate_bias, jnp.zeros((NSM - 2 * M_HEADS,), F32)])[None, :]

    zb, zo, sm, smt = _inproj(x2d, ln1_g[None, :], w, bias, tm=512)
    m_out = _mlstm(zb, zo, sm, smt, mnorm_g[None, :], batch, seq, L=min(128, seq))

    q4 = zb[:, AQ_OFF:KV_OFF].reshape(batch, seq, A_HEADS, dh).transpose(0, 2, 1, 3)
    kv6 = zb[:, KV_OFF:].reshape(batch, seq, 6, G, dh).transpose(2, 0, 3, 1, 4)
    ncp = seq // CMP_STRIDE
    x2 = kv6[0:2].reshape(2, batch * G, ncp, CMP_STRIDE * dh)
    pe = jnp.stack([pe_k, pe_v]).reshape(2, 1, CMP_BLOCK * dh)
    pe = jnp.broadcast_to(pe, (2, 8, CMP_BLOCK * dh)).astype(BF16)
    w1 = jnp.stack([ck_w1, cv_w1]).astype(BF16)
    w2 = jnp.stack([ck_w2, cv_w2]).astype(BF16)
    kvc = _compress(x2, pe, w1, w2)

    kvs = kv6[2:].reshape(4, batch * G, seq, dh)
    expand = (jnp.arange(seq)[None, :] // SEL_BLOCK == jnp.arange(seq // SEL_BLOCK)[:, None]).astype(BF16)
    a_out = _nsa(q4, sm, kvc[0], kvc[1], kvs[0], kvs[1], kvs[2], kvs[3], expand, batch, seq,
                 TQ=128, TK=min(512, seq))

    cw = jnp.concatenate([conv_w, jnp.zeros((8 - conv_w.shape[0], D_FF), F32)], axis=0)
    return _outblock(x2d, m_out, a_out, p2d, w_out.astype(BF16), ln2_g[None, :], w_up.astype(BF16), cw,
                     conv_b[None, :], w_down.astype(BF16), ple_g[None, :], w_pg.astype(BF16),
                     w_pp.astype(BF16), out_g[None, :], seq, tm=256)


def kernel(x, p, ln1_g, w_in, mlstm_gate_bias, mlstm_norm_g, cmp_pos_k, cmp_pos_v, cmp_k_w1, cmp_k_w2,
           cmp_v_w1, cmp_v_w2, w_out, ln2_g, w_up, conv_w, conv_b, w_down, ple_norm_g, w_ple_gate,
           w_ple_proj, final_g):
    batch, seq, d = x.shape
    depth = w_in.shape[0]
    assert depth == 1, "the fused output block applies the final norm, so a single layer is supported"
    assert seq % 128 == 0 and seq >= WINDOW + 128
    x2d = x.reshape(batch * seq, d)
    i = 0
    out = _layer(x2d, p[i].reshape(batch * seq, -1), batch, seq, ln1_g[i], w_in[i], mlstm_gate_bias[i],
                 mlstm_norm_g[i], cmp_pos_k[i], cmp_pos_v[i], cmp_k_w1[i], cmp_k_w2[i], cmp_v_w1[i],
                 cmp_v_w2[i], w_out[i], ln2_g[i], w_up[i], conv_w[i], conv_b[i], w_down[i], ple_norm_g[i],
                 w_ple_gate[i], w_ple_proj[i], final_g)
    return out.reshape(batch, seq, d)
```

```python
import functools

import jax
import jax.numpy as jnp
from jax import lax
from jax.experimental import pallas as pl
from jax.experimental.pallas import tpu as pltpu

F32 = jnp.float32
BF16 = jnp.bfloat16

M_HEADS = 4
M_HEAD_DIM = 128
M_WIDTH = M_HEADS * M_HEAD_DIM
A_HEADS = 8
A_KV_GROUPS = 2
A_REP = A_HEADS // A_KV_GROUPS
A_HEAD_DIM = 64
A_WIDTH = A_HEADS * A_HEAD_DIM
KV_WIDTH = A_KV_GROUPS * A_HEAD_DIM
CMP_BLOCK = 32
CMP_STRIDE = 16
CMP_HIDDEN = 128
SEL_BLOCK = 64
N_SELECT = 16
WINDOW = 512
ALIBI_MAX = 8.0
D_FF = 2048
EPS = 1e-6
NEG = -1e30
BIG = 1e30

LANES = 128
VMEM_LIMIT = 56 * 1024 * 1024

NB = 3 * M_WIDTH + A_WIDTH + 6 * KV_WIDTH
NO = M_WIDTH
NSM = LANES
AQ_OFF = 3 * M_WIDTH
KV_OFF = AQ_OFF + A_WIDTH
AG_OFF = 2 * M_HEADS


def _dot(a, b):
    return jnp.dot(a, b, preferred_element_type=F32)


def _dot_nt(a, b):
    return lax.dot_general(a, b, (((1,), (1,)), ((), ())), preferred_element_type=F32)


def _rms(x, g):
    return x * lax.rsqrt(jnp.mean(x * x, axis=-1, keepdims=True) + EPS) * g


def _log_sigmoid(x):
    return jnp.minimum(x, 0.0) - jnp.log1p(jnp.exp(-jnp.abs(x)))


def _resident(shape):
    zeros = (0,) * len(shape)
    return pl.BlockSpec(shape, lambda *_: zeros, pipeline_mode=pl.Buffered(1))


def _inproj_body(x_ref, g_ref, w_ref, bias_ref, zb_ref, zo_ref, sm_ref, smt_ref):
    h = _rms(x_ref[...], g_ref[...]).astype(BF16)
    cw = 512
    for c0 in range(0, NB, cw):
        c1 = min(c0 + cw, NB)
        zb_ref[:, c0:c1] = _dot(h, w_ref[:, c0:c1]).astype(BF16)
    zo_ref[...] = _dot(h, w_ref[:, NB:NB + NO])
    sm = _dot(h, w_ref[:, NB + NO:]) + bias_ref[...]
    sm_ref[...] = sm
    smt_ref[...] = sm.T[:2 * M_HEADS]


def _inproj(x2d, g, w, bias, tm):
    n, d = x2d.shape
    return pl.pallas_call(
        _inproj_body,
        grid=(n // tm,),
        in_specs=[
            pl.BlockSpec((tm, d), lambda i: (i, 0)),
            _resident((1, d)),
            _resident(w.shape),
            _resident((1, NSM)),
        ],
        out_specs=[
            pl.BlockSpec((tm, NB), lambda i: (i, 0)),
            pl.BlockSpec((tm, NO), lambda i: (i, 0)),
            pl.BlockSpec((tm, NSM), lambda i: (i, 0)),
            pl.BlockSpec((2 * M_HEADS, tm), lambda i: (0, i)),
        ],
        out_shape=[
            jax.ShapeDtypeStruct((n, NB), BF16),
            jax.ShapeDtypeStruct((n, NO), F32),
            jax.ShapeDtypeStruct((n, NSM), F32),
            jax.ShapeDtypeStruct((2 * M_HEADS, n), F32),
        ],
        compiler_params=pltpu.CompilerParams(
            dimension_semantics=("arbitrary",), vmem_limit_bytes=VMEM_LIMIT),
    )(x2d, g, w, bias)


def _mlstm_body(zb_ref, zo_ref, sm_ref, smt_ref, gn_ref, out_ref, c_ref, n_ref, m_ref, *, L):
    @pl.when(pl.program_id(1) == 0)
    def _():
        c_ref[...] = jnp.zeros_like(c_ref)
        n_ref[...] = jnp.zeros_like(n_ref)
        m_ref[...] = jnp.zeros_like(m_ref)

    dh = M_HEAD_DIM
    scale = dh ** -0.5
    li = lax.broadcasted_iota(jnp.int32, (L, L), 0)
    si = lax.broadcasted_iota(jnp.int32, (L, L), 1)
    causal = si <= li
    sm = sm_ref[...]
    smt = smt_ref[...]
    for h in range(M_HEADS):
        q = zb_ref[:, h * dh:(h + 1) * dh]
        k = zb_ref[:, M_WIDTH + h * dh:M_WIDTH + (h + 1) * dh]
        v = zb_ref[:, 2 * M_WIDTH + h * dh:2 * M_WIDTH + (h + 1) * dh]
        ig_col = sm[:, h:h + 1]
        ig_row = smt[h:h + 1, :]
        lf_col = _log_sigmoid(sm[:, M_HEADS + h:M_HEADS + h + 1])
        lf_row = _log_sigmoid(smt[M_HEADS + h:M_HEADS + h + 1, :])
        b_col = jnp.sum(jnp.where(causal, lf_row, 0.0), axis=1, keepdims=True)
        b_row = jnp.sum(jnp.where(li <= si, lf_col, 0.0), axis=0, keepdims=True)
        b_last = jnp.sum(lf_row, axis=1, keepdims=True)
        m_prev = m_ref[h][:, :1]

        dm = jnp.where(causal, b_col - b_row + ig_row, -jnp.inf)
        bq = b_col + m_prev
        m_loc = jnp.maximum(bq, jnp.max(dm, axis=1, keepdims=True))
        inter = jnp.exp(bq - m_loc)
        wt = jnp.exp(dm - m_loc) * _dot_nt(q, k)
        c_prev = c_ref[h]
        n_prev = n_ref[h]
        num = inter * _dot(q, c_prev.astype(BF16)) + _dot(wt.astype(BF16), v)
        den = (inter * jnp.sum(q.astype(F32) * n_prev, axis=1, keepdims=True)
               + jnp.sum(wt, axis=1, keepdims=True))
        hh = (num * scale) / jnp.maximum(jnp.abs(den) * scale, jnp.exp(-m_loc))
        hn = _rms(hh, gn_ref[:, h * dh:(h + 1) * dh])
        o = zo_ref[:, h * dh:(h + 1) * dh]
        out_ref[:, h * dh:(h + 1) * dh] = (hn * jax.nn.sigmoid(o)).astype(out_ref.dtype)

        a_col = b_last - b_col + ig_col
        a_row = b_last - b_row + ig_row
        m_new = jnp.maximum(b_last + m_prev, jnp.max(a_row, axis=1, keepdims=True))
        decay = jnp.exp(b_last + m_prev - m_new)
        kf = k.astype(F32)
        kw_t = (kf.T * jnp.exp(a_row - m_new)).astype(BF16)
        c_ref[h] = decay * c_prev + _dot(kw_t, v)
        n_ref[h] = decay * n_prev + jnp.sum(kf * jnp.exp(a_col - m_new), axis=0, keepdims=True)
        m_ref[h] = jnp.broadcast_to(m_new, (1, LANES))


def _mlstm(zb, zo, sm, smt, gn, batch, seq, L):
    n = batch * seq
    nc = seq // L
    return pl.pallas_call(
        functools.partial(_mlstm_body, L=L),
        grid=(batch, nc),
        in_specs=[
            pl.BlockSpec((L, 3 * M_WIDTH), lambda b, c: (b * nc + c, 0)),
            pl.BlockSpec((L, NO), lambda b, c: (b * nc + c, 0)),
            pl.BlockSpec((L, NSM), lambda b, c: (b * nc + c, 0)),
            pl.BlockSpec((2 * M_HEADS, L), lambda b, c: (0, b * nc + c)),
            _resident((1, M_WIDTH)),
        ],
        out_specs=pl.BlockSpec((L, M_WIDTH), lambda b, c: (b * nc + c, 0)),
        out_shape=jax.ShapeDtypeStruct((n, M_WIDTH), BF16),
        scratch_shapes=[
            pltpu.VMEM((M_HEADS, M_HEAD_DIM, M_HEAD_DIM), F32),
            pltpu.VMEM((M_HEADS, 1, M_HEAD_DIM), F32),
            pltpu.VMEM((M_HEADS, 1, LANES), F32),
        ],
        compiler_params=pltpu.CompilerParams(
            dimension_semantics=("arbitrary", "arbitrary"), vmem_limit_bytes=VMEM_LIMIT),
    )(zb, zo, sm, smt, gn)


def _compress_body(x_ref, pe_ref, w1_ref, w2_ref, out_ref, *, ncp):
    half = (CMP_BLOCK // 2) * A_HEAD_DIM
    x = x_ref[...]
    first = _dot(x, w1_ref[:half, :])
    second = _dot(x, w1_ref[half:, :])
    second = pltpu.roll(second, ncp - 1, 0)
    pe_term = _dot(pe_ref[...], w1_ref[...])[0:1]
    hid = jax.nn.gelu(first + second + pe_term)
    out_ref[...] = _dot(hid.astype(BF16), w2_ref[...]).astype(out_ref.dtype)


def _compress(x2, pe, w1, w2):
    two, bg, ncp, width = x2.shape
    return pl.pallas_call(
        functools.partial(_compress_body, ncp=ncp),
        grid=(two, bg),
        in_specs=[
            pl.BlockSpec((None, None, ncp, width), lambda a, i: (a, i, 0, 0)),
            pl.BlockSpec((None, 8, CMP_BLOCK * A_HEAD_DIM), lambda a, i: (a, 0, 0)),
            pl.BlockSpec((None, CMP_BLOCK * A_HEAD_DIM, CMP_HIDDEN), lambda a, i: (a, 0, 0)),
            pl.BlockSpec((None, CMP_HIDDEN, A_HEAD_DIM), lambda a, i: (a, 0, 0)),
        ],
        out_specs=pl.BlockSpec((None, None, ncp, A_HEAD_DIM), lambda a, i: (a, i, 0, 0)),
        out_shape=jax.ShapeDtypeStruct((two, bg, ncp, A_HEAD_DIM), BF16),
        compiler_params=pltpu.CompilerParams(
            dimension_semantics=("arbitrary", "arbitrary"), vmem_limit_bytes=VMEM_LIMIT),
    )(x2, pe, w1, w2)


def _nsa_body(qt_ref, sm_ref, kc_ref, vct_ref, ks_ref, vst_ref, kw_ref, vwt_ref, out_ref, *, TQ, TK, S, NCP):
    g = pl.program_id(1)
    t0 = pl.program_id(2) * TQ
    R = A_REP
    M = R * TQ
    D = A_HEAD_DIM
    NS = S // SEL_BLOCK
    n_sel = min(N_SELECT, NS)
    lanes = [slice(r * TQ, (r + 1) * TQ) for r in range(R)]
    dt = lax.broadcasted_iota(jnp.int32, (1, TQ), 1)

    slope = jnp.zeros((1, M), F32)
    head = lax.broadcasted_iota(jnp.int32, (1, M), 1) >> (TQ.bit_length() - 1)
    for r in range(R):
        s_r = jnp.where(g == 0, 2.0 ** (-ALIBI_MAX * (r + 1) / A_HEADS),
                        2.0 ** (-ALIBI_MAX * (R + r + 1) / A_HEADS)).astype(F32)
        slope = jnp.where(head == r, s_r, slope)
    row8 = lax.broadcasted_iota(jnp.int32, (8, M), 0)
    feat = jnp.where(row8 == 0, slope * 256.0, jnp.where(row8 == 1, slope, 0.0))
    q_f = qt_ref[...].astype(F32)
    q_plain = jnp.concatenate([q_f, feat, jnp.zeros((LANES - D - 8, M), F32)], axis=0).astype(BF16)

    def softmax_cols(s_t, mask_t, m_old):
        ps, ms, alphas = [], [], []
        for r in range(R):
            blk = s_t[:, lanes[r]]
            if mask_t is not None:
                blk = blk + mask_t
            m_new = jnp.max(blk, axis=0, keepdims=True)
            if m_old is not None:
                m_new = jnp.maximum(m_new, m_old[:, lanes[r]])
                alphas.append(jnp.exp(m_old[:, lanes[r]] - m_new))
            ps.append(jnp.exp(blk - m_new).astype(BF16))
            ms.append(m_new)
        cat = lambda xs: jnp.concatenate(xs, axis=1)
        return cat(ps), cat(ms), (cat(alphas) if alphas else None)

    wk = min(WINDOW + TQ, S)
    k0 = pl.multiple_of(jnp.maximum(t0 + TQ - wk, 0), TQ)
    s_w = _dot(kw_ref[pl.ds(k0, wk), :], q_plain)
    dist = dt - (k0 - t0 + lax.broadcasted_iota(jnp.int32, (wk, 1), 0))
    mask_w = jnp.where((dist >= 0) & (dist < WINDOW), 0.0, NEG)
    p_w, _, _ = softmax_cols(s_w, mask_w, None)
    v_band = vwt_ref[pl.ds(k0 // TQ, wk // TQ)]
    acc_w = _dot(jnp.concatenate([v_band[j] for j in range(wk // TQ)], axis=1), p_w)
    o_w = acc_w[:D] * (1.0 / acc_w[D:D + 1])

    cend = lax.broadcasted_iota(jnp.int32, (NCP, 1), 0) * CMP_STRIDE + (CMP_BLOCK - 1) - t0
    mask_c = jnp.where(cend <= dt, 0.0, NEG)
    s_c = _dot(kc_ref[...], q_plain)
    p_sum = jnp.zeros((NCP, TQ), F32)
    p_c = []
    for r in range(R):
        blk = s_c[:, lanes[r]] + mask_c
        mx = jnp.maximum(jnp.max(blk, axis=0, keepdims=True), 0.1 * NEG)
        e = jnp.exp(blk - mx)
        p = e * (1.0 / jnp.maximum(jnp.sum(e, axis=0, keepdims=True), 1e-30))
        p_sum = p_sum + p
        p_c.append(p.astype(BF16))
    o_c = _dot(vct_ref[...], jnp.concatenate(p_c, axis=1))

    jc = lax.broadcasted_iota(jnp.int32, (NS, 1), 0)
    nr = lax.broadcasted_iota(jnp.int32, (1, NCP), 1)
    overlap_t = ((nr * CMP_STRIDE <= jc * SEL_BLOCK + (SEL_BLOCK - 1))
                 & (nr * CMP_STRIDE + (CMP_BLOCK - 1) >= jc * SEL_BLOCK))
    imp_t = _dot(jnp.where(overlap_t, 1.0, 0.0).astype(BF16), p_sum.astype(BF16))
    cur = (t0 + dt) >> (SEL_BLOCK.bit_length() - 1)
    forced = (jc == 0) | (jc == cur) | (jc == cur - 1)
    score = jnp.where(jc <= cur, jnp.where(forced, BIG, imp_t), NEG)
    sub = lax.broadcasted_iota(jnp.int32, (8, TQ), 0)
    blocks = [score[8 * jb:8 * jb + 8] for jb in range(NS // 8)]
    ranks = [jnp.zeros((8, TQ), F32) for _ in blocks]
    for i in range(NS):
        s_i = score[i:i + 1, :]
        for jb, blk in enumerate(blocks):
            if 8 * jb > i:
                beats = jnp.where(s_i >= blk, 1.0, 0.0)
            elif 8 * jb + 7 < i:
                beats = jnp.where(s_i > blk, 1.0, 0.0)
            else:
                beats = jnp.where(sub + 8 * jb > i, jnp.where(s_i >= blk, 1.0, 0.0), jnp.where(s_i > blk, 1.0, 0.0))
            ranks[jb] = ranks[jb] + beats
    sel_t = jnp.where(jnp.concatenate(ranks, axis=0) < n_sel, 0.0, NEG)
    sel_rows = [jnp.concatenate([sel_t] * R, axis=1)]
    if NS < LANES - D:
        sel_rows.append(jnp.zeros((LANES - D - NS, M), F32))
    q_sel = jnp.concatenate([q_f] + sel_rows + [feat, jnp.zeros((LANES - 8, M), F32)], axis=0).astype(BF16)

    def scores(kt):
        return _dot(ks_ref[pl.ds(pl.multiple_of(kt * TK, TK), TK), :], q_sel)

    def sel_tile(kt, carry):
        m_all, acc, s_t = carry
        s_next = scores(kt + 1)
        p_t, m_new, alpha = softmax_cols(s_t, None, m_all)
        return m_new, alpha * acc + _dot(vst_ref[kt], p_t), s_next

    kd = t0 // TK
    init = (jnp.full((1, M), NEG, F32), jnp.zeros((LANES, M), F32), scores(0))
    m_all, acc_s, s_t = lax.fori_loop(0, kd, sel_tile, init)
    mask_d = jnp.where(kd * TK - t0 + lax.broadcasted_iota(jnp.int32, (TK, 1), 0) <= dt, 0.0, NEG)
    p_t, _, alpha = softmax_cols(s_t, mask_d, m_all)
    acc_s = alpha * acc_s + _dot(vst_ref[kd], p_t)
    o_s = acc_s[:D] * (1.0 / acc_s[D:D + 1])

    sm_t = sm_ref[...].T
    mixed = []
    for r in range(R):
        gates = []
        for c in range(3):
            c0 = AG_OFF + r * 3 + c
            c1 = AG_OFF + (R + r) * 3 + c
            gates.append(jax.nn.sigmoid(jnp.where(g == 0, sm_t[c0:c0 + 1, :], sm_t[c1:c1 + 1, :])))
        mixed.append(gates[0] * o_c[:, lanes[r]] + gates[1] * o_s[:, lanes[r]] + gates[2] * o_w[:, lanes[r]])
    for r in range(0, R, 2):
        pair = jnp.concatenate([mixed[r], mixed[r + 1]], axis=0).T
        out_ref[:, r * D:(r + 2) * D] = pair.astype(out_ref.dtype)


def _nsa(qt, sm, kc, vct, ks, vst, kw, vwt, batch, seq, TQ, TK):
    n = batch * seq
    G = A_KV_GROUPS
    nq = seq // TQ
    ncp = kc.shape[1]
    per_group = lambda a: pl.BlockSpec((None,) + a.shape[1:], lambda b, g, i: (b * G + g,) + (0,) * (a.ndim - 1))
    return pl.pallas_call(
        functools.partial(_nsa_body, TQ=TQ, TK=TK, S=seq, NCP=ncp),
        grid=(batch, G, nq),
        in_specs=[
            pl.BlockSpec((None, None, None, A_HEAD_DIM, A_REP * TQ), lambda b, g, i: (b, g, i, 0, 0)),
            pl.BlockSpec((TQ, NSM), lambda b, g, i: (b * nq + i, 0)),
            per_group(kc), per_group(vct), per_group(ks), per_group(vst), per_group(kw), per_group(vwt),
        ],
        out_specs=pl.BlockSpec((TQ, A_REP * A_HEAD_DIM), lambda b, g, i: (b * nq + i, g)),
        out_shape=jax.ShapeDtypeStruct((n, A_WIDTH), BF16),
        compiler_params=pltpu.CompilerParams(
            dimension_semantics=("arbitrary", "arbitrary", "arbitrary"), vmem_limit_bytes=VMEM_LIMIT),
    )(qt, sm, kc, vct, ks, vst, kw, vwt)


def _outblock_body(x_ref, mo_ref, ao_ref, p_ref, wout_ref, ln2_ref, wup_ref, cw_ref, cb_ref, wdn_ref,
                   pg_ref, wpg_ref, wpp_ref, fg_ref, out_ref, tail_ref, *, tm, tiles_per_seq, fc):
    @pl.when(pl.program_id(0) % tiles_per_seq == 0)
    def _():
        tail_ref[...] = jnp.zeros_like(tail_ref)

    x1 = (x_ref[...] + _dot(mo_ref[...], wout_ref[:M_WIDTH, :]) + _dot(ao_ref[...], wout_ref[M_WIDTH:, :]))
    h2 = _rms(x1, ln2_ref[...]).astype(BF16)
    rowi = lax.broadcasted_iota(jnp.int32, (tm, 1), 0)
    acc = jnp.zeros(x1.shape, F32)
    for c0 in range(0, D_FF, fc):
        gate = _dot(h2, wup_ref[:, c0:c0 + fc])
        val = _dot(h2, wup_ref[:, D_FF + c0:D_FF + c0 + fc])
        tail = tail_ref[:, c0:c0 + fc]
        g_m1 = jnp.where(rowi == 0, tail[7:8], pltpu.roll(gate, 1, 0))
        g_m2 = jnp.where(rowi == 0, tail[6:7], jnp.where(rowi == 1, tail[7:8], pltpu.roll(gate, 2, 0)))
        conv = (cw_ref[0:1, c0:c0 + fc] * g_m2 + cw_ref[1:2, c0:c0 + fc] * g_m1
                + cw_ref[2:3, c0:c0 + fc] * gate + cb_ref[:, c0:c0 + fc])
        y = jax.nn.silu(conv) * val
        acc = acc + _dot(y.astype(BF16), wdn_ref[c0:c0 + fc, :])
        tail_ref[:, c0:c0 + fc] = gate[tm - 8:tm]
    x2 = x1 + acc
    gate2 = jax.nn.sigmoid(_dot(_rms(x2, pg_ref[...]).astype(BF16), wpg_ref[...]))
    x3 = x2 + gate2 * _dot(p_ref[...].astype(BF16), wpp_ref[...])
    out_ref[...] = _rms(x3, fg_ref[...])


def _outblock(x2d, mo, ao, p2d, wout, ln2, wup, cw, cb, wdn, pg, wpg, wpp, fg, seq, tm):
    n, d = x2d.shape
    row = lambda width: pl.BlockSpec((tm, width), lambda i: (i, 0))
    return pl.pallas_call(
        functools.partial(_outblock_body, tm=tm, tiles_per_seq=seq // tm, fc=512),
        grid=(n // tm,),
        in_specs=[
            row(d), row(M_WIDTH), row(A_WIDTH), row(p2d.shape[1]),
            _resident(wout.shape), _resident((1, d)), _resident(wup.shape), _resident(cw.shape),
            _resident((1, D_FF)), _resident(wdn.shape), _resident((1, d)), _resident(wpg.shape),
            _resident(wpp.shape), _resident((1, d)),
        ],
        out_specs=row(d),
        out_shape=jax.ShapeDtypeStruct((n, d), F32),
        scratch_shapes=[pltpu.VMEM((8, D_FF), F32)],
        compiler_params=pltpu.CompilerParams(
            dimension_semantics=("arbitrary",), vmem_limit_bytes=VMEM_LIMIT),
    )(x2d, mo, ao, p2d, wout, ln2, wup, cw, cb, wdn, pg, wpg, wpp, fg)


def _layer(x2d, p2d, batch, seq, ln1_g, w_in, gate_bias, mnorm_g, pe_k, pe_v, ck_w1, ck_w2, cv_w1, cv_w2,
           w_out, ln2_g, w_up, conv_w, conv_b, w_down, ple_g, w_pg, w_pp, out_g):
    d = x2d.shape[1]
    G = A_KV_GROUPS
    dh = A_HEAD_DIM
    o_mo = 3 * M_WIDTH
    o_mi = o_mo + M_WIDTH
    o_aq = o_mi + 2 * M_HEADS
    o_ag = o_aq + A_WIDTH + 6 * KV_WIDTH
    w = jnp.concatenate([
        w_in[:, :o_mo], w_in[:, o_aq:o_aq + A_WIDTH] * (dh ** -0.5), w_in[:, o_aq + A_WIDTH:o_ag],
        w_in[:, o_mo:o_mi],
        w_in[:, o_mi:o_aq], w_in[:, o_ag:], jnp.zeros((d, NSM - 2 * M_HEADS - 3 * A_HEADS), w_in.dtype),
    ], axis=1).astype(BF16)
    bias = jnp.concatenate([gate_bias, jnp.zeros((NSM - 2 * M_HEADS,), F32)])[None, :]

    zb, zo, sm, smt = _inproj(x2d, ln1_g[None, :], w, bias, tm=512)
    m_out = _mlstm(zb, zo, sm, smt, mnorm_g[None, :], batch, seq, L=min(128, seq))

    TQ, TK = 128, min(512, seq)
    nq = seq // TQ
    qt = (zb[:, AQ_OFF:KV_OFF].reshape(batch, nq, TQ, G, A_REP, dh).transpose(0, 3, 1, 5, 4, 2)
          .reshape(batch, G, nq, dh, A_REP * TQ))
    kv6 = zb[:, KV_OFF:].reshape(batch, seq, 6, G, dh).transpose(2, 0, 3, 1, 4)
    ncp = seq // CMP_STRIDE
    x2 = kv6[0:2].reshape(2, batch * G, ncp, CMP_STRIDE * dh)
    pe = jnp.stack([pe_k, pe_v]).reshape(2, 1, CMP_BLOCK * dh)
    pe = jnp.broadcast_to(pe, (2, 8, CMP_BLOCK * dh)).astype(BF16)
    w1 = jnp.stack([ck_w1, cv_w1]).astype(BF16)
    w2 = jnp.stack([ck_w2, cv_w2]).astype(BF16)
    kvc = _compress(x2, pe, w1, w2)

    bg = batch * G
    kvs = kv6[2:].reshape(4, bg, seq, dh)
    tile = lambda a: jnp.broadcast_to(a.astype(BF16), (bg,) + a.shape)
    pos = jnp.arange(seq)[:, None]
    pos_feat = jnp.concatenate([pos // 256, pos % 256], axis=1)
    onehot = pos // SEL_BLOCK == jnp.arange(LANES - dh)[None, :]
    ks_aug = jnp.concatenate([kvs[0], tile(onehot), tile(pos_feat), jnp.zeros((bg, seq, LANES - 2), BF16)], axis=-1)
    kw_aug = jnp.concatenate([kvs[2], tile(pos_feat), jnp.zeros((bg, seq, LANES - dh - 2), BF16)], axis=-1)
    cpos = jnp.concatenate([jnp.zeros((ncp, 1), jnp.int32), jnp.arange(ncp)[:, None] * CMP_STRIDE], axis=1)
    kc_aug = jnp.concatenate([kvc[0], tile(cpos), jnp.zeros((bg, ncp, LANES - dh - 2), BF16)], axis=-1)
    ones = jnp.zeros((bg, seq, LANES - dh), BF16).at[:, :, 0].set(1.0)
    as_tiles = lambda v, t: (jnp.concatenate([v, ones], axis=-1).reshape(bg, seq // t, t, LANES)
                             .transpose(0, 1, 3, 2))
    a_out = _nsa(qt, sm, kc_aug, kvc[1].transpose(0, 2, 1), ks_aug, as_tiles(kvs[1], TK), kw_aug,
                 as_tiles(kvs[3], TQ), batch, seq, TQ=TQ, TK=TK)

    cw = jnp.concatenate([conv_w, jnp.zeros((8 - conv_w.shape[0], D_FF), F32)], axis=0)
    return _outblock(x2d, m_out, a_out, p2d, w_out.astype(BF16), ln2_g[None, :], w_up.astype(BF16), cw,
                     conv_b[None, :], w_down.astype(BF16), ple_g[None, :], w_pg.astype(BF16),
                     w_pp.astype(BF16), out_g[None, :], seq, tm=256)


def kernel(x, p, ln1_g, w_in, mlstm_gate_bias, mlstm_norm_g, cmp_pos_k, cmp_pos_v, cmp_k_w1, cmp_k_w2,
           cmp_v_w1, cmp_v_w2, w_out, ln2_g, w_up, conv_w, conv_b, w_down, ple_norm_g, w_ple_gate,
           w_ple_proj, final_g):
    batch, seq, d = x.shape
    depth = w_in.shape[0]
    assert depth == 1, "the fused output block applies the final norm, so a single layer is supported"
    assert seq % 128 == 0 and WINDOW + 128 <= seq <= SEL_BLOCK * (LANES - A_HEAD_DIM)
    x2d = x.reshape(batch * seq, d)
    i = 0
    out = _layer(x2d, p[i].reshape(batch * seq, -1), batch, seq, ln1_g[i], w_in[i], mlstm_gate_bias[i],
                 mlstm_norm_g[i], cmp_pos_k[i], cmp_pos_v[i], cmp_k_w1[i], cmp_k_w2[i], cmp_v_w1[i],
                 cmp_v_w2[i], w_out[i], ln2_g[i], w_up[i], conv_w[i], conv_b[i], w_down[i], ple_norm_g[i],
                 w_ple_gate[i], w_ple_proj[i], final_g)
    return out.reshape(batch, seq, d)
```

```python
import functools

import jax
import jax.numpy as jnp
from jax import lax
from jax.experimental import pallas as pl
from jax.experimental.pallas import tpu as pltpu

F32 = jnp.float32
BF16 = jnp.bfloat16

M_HEADS = 4
M_HEAD_DIM = 128
M_WIDTH = M_HEADS * M_HEAD_DIM
A_HEADS = 8
A_KV_GROUPS = 2
A_REP = A_HEADS // A_KV_GROUPS
A_HEAD_DIM = 64
A_WIDTH = A_HEADS * A_HEAD_DIM
KV_WIDTH = A_KV_GROUPS * A_HEAD_DIM
CMP_BLOCK = 32
CMP_STRIDE = 16
CMP_HIDDEN = 128
SEL_BLOCK = 64
N_SELECT = 16
WINDOW = 512
ALIBI_MAX = 8.0
D_FF = 2048
EPS = 1e-6
NEG = -1e30
BIG = 1e30

LANES = 128
VMEM_LIMIT = 56 * 1024 * 1024

NB = 3 * M_WIDTH + A_WIDTH + 6 * KV_WIDTH
NO = M_WIDTH
NSM = LANES
AQ_OFF = 3 * M_WIDTH
KV_OFF = AQ_OFF + A_WIDTH
AG_OFF = 2 * M_HEADS


def _dot(a, b):
    return jnp.dot(a, b, preferred_element_type=F32)


def _dot_nt(a, b):
    return lax.dot_general(a, b, (((1,), (1,)), ((), ())), preferred_element_type=F32)


def _rms(x, g):
    return x * lax.rsqrt(jnp.mean(x * x, axis=-1, keepdims=True) + EPS) * g


def _log_sigmoid(x):
    return jnp.minimum(x, 0.0) - jnp.log1p(jnp.exp(-jnp.abs(x)))


def _resident(shape):
    zeros = (0,) * len(shape)
    return pl.BlockSpec(shape, lambda *_: zeros, pipeline_mode=pl.Buffered(1))


def _inproj_body(x_ref, g_ref, w_ref, bias_ref, tri_ref, zm_ref, zq_ref, zkv_ref, zo_ref, sm_ref, smt_ref):
    h = _rms(x_ref[...], g_ref[...]).astype(BF16)
    sm_t = (_dot(h, w_ref[:, NB + NO:]) + bias_ref[...]).T
    rows = 4 * M_HEADS
    gate = sm_t[:rows]
    row = lax.broadcasted_iota(jnp.int32, gate.shape, 0)
    is_f = (row >= M_HEADS) & (row < 2 * M_HEADS)
    rem = _log_sigmoid(gate)
    parts = []
    for _ in range(3):
        parts.append(rem.astype(BF16))
        rem = rem - parts[-1].astype(F32)

    cw = 512
    for ref, base in ((zm_ref, 0), (zq_ref, AQ_OFF), (zkv_ref, KV_OFF)):
        for c0 in range(0, ref.shape[1], cw):
            c1 = min(c0 + cw, ref.shape[1])
            ref[:, c0:c1] = _dot(h, w_ref[:, base + c0:base + c1]).astype(BF16)
    zo_ref[...] = _dot(h, w_ref[:, NB:NB + NO])

    cum = _dot(parts[0], tri_ref[...]) + _dot(parts[1], tri_ref[...]) + _dot(parts[2], tri_ref[...])
    gate = jnp.where(is_f, cum, gate)
    smt_ref[...] = gate[:2 * M_HEADS]
    sm_ref[...] = jnp.concatenate([gate, sm_t[rows:]], axis=0).T


def _inproj(x2d, g, w, bias, tm, chunk):
    n, d = x2d.shape
    idx = jnp.arange(tm)
    tri = ((idx[:, None] // chunk == idx[None, :] // chunk) & (idx[:, None] <= idx[None, :])).astype(BF16)
    return pl.pallas_call(
        _inproj_body,
        grid=(n // tm,),
        in_specs=[
            pl.BlockSpec((tm, d), lambda i: (i, 0)),
            _resident((1, d)),
            _resident(w.shape),
            _resident((1, NSM)),
            _resident((tm, tm)),
        ],
        out_specs=[
            pl.BlockSpec((tm, AQ_OFF), lambda i: (i, 0)),
            pl.BlockSpec((tm, KV_OFF - AQ_OFF), lambda i: (i, 0)),
            pl.BlockSpec((tm, NB - KV_OFF), lambda i: (i, 0)),
            pl.BlockSpec((tm, NO), lambda i: (i, 0)),
            pl.BlockSpec((tm, NSM), lambda i: (i, 0)),
            pl.BlockSpec((2 * M_HEADS, tm), lambda i: (0, i)),
        ],
        out_shape=[
            jax.ShapeDtypeStruct((n, AQ_OFF), BF16),
            jax.ShapeDtypeStruct((n, KV_OFF - AQ_OFF), BF16),
            jax.ShapeDtypeStruct((n, NB - KV_OFF), BF16),
            jax.ShapeDtypeStruct((n, NO), F32),
            jax.ShapeDtypeStruct((n, NSM), F32),
            jax.ShapeDtypeStruct((2 * M_HEADS, n), F32),
        ],
        compiler_params=pltpu.CompilerParams(
            dimension_semantics=("arbitrary",), vmem_limit_bytes=VMEM_LIMIT),
    )(x2d, g, w, bias, tri)


def _mlstm_body(zb_ref, zo_ref, sm_ref, *rest, L, NBT):
    smt_refs, (gn_ref, out_ref, c_ref, m_ref) = rest[:NBT], rest[NBT:]

    @pl.when(pl.program_id(1) == 0)
    def _():
        c_ref[...] = jnp.zeros_like(c_ref)
        m_ref[...] = jnp.zeros_like(m_ref)

    dh = M_HEAD_DIM
    scale = dh ** -0.5
    causal = lax.broadcasted_iota(jnp.int32, (L, L), 1) <= lax.broadcasted_iota(jnp.int32, (L, L), 0)
    chains = [(bi, h) for bi in range(NBT) for h in range(M_HEADS)]
    each = lambda f: [f(i, bi, h) for i, (bi, h) in enumerate(chains)]
    col = lambda h: slice(h * dh, (h + 1) * dh)
    ones = (lax.broadcasted_iota(jnp.int32, (L, LANES), 1) == 0).astype(BF16)

    q = each(lambda i, bi, h: zb_ref[bi, :, col(h)])
    k = each(lambda i, bi, h: zb_ref[bi, :, M_WIDTH + h * dh:M_WIDTH + (h + 1) * dh])
    v = each(lambda i, bi, h: jnp.concatenate([zb_ref[bi, :, 2 * M_WIDTH + h * dh:2 * M_WIDTH + (h + 1) * dh],
                                               ones], axis=1))
    c_prev = each(lambda i, bi, h: c_ref[i])
    qk = each(lambda i, bi, h: _dot_nt(q[i], k[i]))
    qc = each(lambda i, bi, h: _dot(q[i], c_prev[i].astype(BF16)))

    ig_col = each(lambda i, bi, h: sm_ref[bi, :, h:h + 1])
    b_col = each(lambda i, bi, h: sm_ref[bi, :, M_HEADS + h:M_HEADS + h + 1])
    ig_row = each(lambda i, bi, h: smt_refs[bi][h:h + 1, :])
    b_row = each(lambda i, bi, h: smt_refs[bi][M_HEADS + h:M_HEADS + h + 1, :])
    b_last = each(lambda i, bi, h: b_row[i][:, L - 1:L])
    m_prev = each(lambda i, bi, h: m_ref[i][:, :1])

    dm = each(lambda i, bi, h: jnp.where(causal, b_col[i] - b_row[i] + ig_row[i], -jnp.inf))
    bq = each(lambda i, bi, h: b_col[i] + m_prev[i])
    dm_max = each(lambda i, bi, h: jnp.max(dm[i], axis=1, keepdims=True))
    m_loc = each(lambda i, bi, h: jnp.maximum(bq[i], dm_max[i]))
    inter = each(lambda i, bi, h: jnp.exp(bq[i] - m_loc[i]))
    wt = each(lambda i, bi, h: (jnp.exp(dm[i] - m_loc[i]) * qk[i]).astype(BF16))

    a_row = each(lambda i, bi, h: b_last[i] - b_row[i] + ig_row[i])
    a_max = each(lambda i, bi, h: jnp.max(a_row[i], axis=1, keepdims=True))
    m_new = each(lambda i, bi, h: jnp.maximum(b_last[i] + m_prev[i], a_max[i]))
    decay = each(lambda i, bi, h: jnp.exp(b_last[i] + m_prev[i] - m_new[i]))
    k_t = each(lambda i, bi, h: k[i].astype(F32).T)
    kw_t = each(lambda i, bi, h: (k_t[i] * jnp.exp(a_row[i] - m_new[i])).astype(BF16))

    wv = each(lambda i, bi, h: _dot(wt[i], v[i]))
    kv = each(lambda i, bi, h: _dot(kw_t[i], v[i]))
    for i in range(len(chains)):
        c_ref[i] = decay[i] * c_prev[i] + kv[i]
        m_ref[i] = jnp.broadcast_to(m_new[i], (1, LANES))

    tot = each(lambda i, bi, h: inter[i] * qc[i] + wv[i])
    floor = each(lambda i, bi, h: jnp.exp(-m_loc[i]))
    hh = each(lambda i, bi, h: (tot[i][:, :dh] * scale)
              / jnp.maximum(jnp.abs(tot[i][:, dh:dh + 1]) * scale, floor[i]))
    ms = each(lambda i, bi, h: jnp.mean(hh[i] * hh[i], axis=-1, keepdims=True))
    for i, (bi, h) in enumerate(chains):
        hn = hh[i] * lax.rsqrt(ms[i] + EPS) * gn_ref[:, col(h)]
        out_ref[bi, :, col(h)] = (hn * jax.nn.sigmoid(zo_ref[bi, :, col(h)])).astype(out_ref.dtype)


def _mlstm(zb, zo, sm, smt, gn, batch, seq, L, NBT):
    nc = seq // L
    rows = lambda width: pl.BlockSpec((NBT, L, width), lambda b, c: (b, c, 0))
    smt_specs = [pl.BlockSpec((2 * M_HEADS, L), functools.partial(lambda b, c, j: (0, (b * NBT + j) * nc + c), j=j))
                 for j in range(NBT)]
    nchain = NBT * M_HEADS
    out = pl.pallas_call(
        functools.partial(_mlstm_body, L=L, NBT=NBT),
        grid=(batch // NBT, nc),
        in_specs=[rows(zb.shape[-1]), rows(NO), rows(NSM)] + smt_specs + [_resident((1, M_WIDTH))],
        out_specs=rows(M_WIDTH),
        out_shape=jax.ShapeDtypeStruct((batch, seq, M_WIDTH), BF16),
        scratch_shapes=[
            pltpu.VMEM((nchain, M_HEAD_DIM, 2 * M_HEAD_DIM), F32),
            pltpu.VMEM((nchain, 1, LANES), F32),
        ],
        compiler_params=pltpu.CompilerParams(
            dimension_semantics=("arbitrary", "arbitrary"), vmem_limit_bytes=VMEM_LIMIT),
    )(zb.reshape(batch, seq, -1), zo.reshape(batch, seq, -1), sm.reshape(batch, seq, -1), *([smt] * NBT), gn)
    return out.reshape(batch * seq, M_WIDTH)


def _compress_body(x_ref, pe_ref, w1_ref, w2_ref, out_ref, *, ncp):
    half = (CMP_BLOCK // 2) * A_HEAD_DIM
    x = x_ref[...]
    first = _dot(x, w1_ref[:half, :])
    second = _dot(x, w1_ref[half:, :])
    second = pltpu.roll(second, ncp - 1, 0)
    pe_term = _dot(pe_ref[...], w1_ref[...])[0:1]
    hid = jax.nn.gelu(first + second + pe_term)
    out_ref[...] = _dot(hid.astype(BF16), w2_ref[...]).astype(out_ref.dtype)


def _compress(x2, pe, w1, w2):
    two, bg, ncp, width = x2.shape
    return pl.pallas_call(
        functools.partial(_compress_body, ncp=ncp),
        grid=(two, bg),
        in_specs=[
            pl.BlockSpec((None, None, ncp, width), lambda a, i: (a, i, 0, 0)),
            pl.BlockSpec((None, 8, CMP_BLOCK * A_HEAD_DIM), lambda a, i: (a, 0, 0)),
            pl.BlockSpec((None, CMP_BLOCK * A_HEAD_DIM, CMP_HIDDEN), lambda a, i: (a, 0, 0)),
            pl.BlockSpec((None, CMP_HIDDEN, A_HEAD_DIM), lambda a, i: (a, 0, 0)),
        ],
        out_specs=pl.BlockSpec((None, None, ncp, A_HEAD_DIM), lambda a, i: (a, i, 0, 0)),
        out_shape=jax.ShapeDtypeStruct((two, bg, ncp, A_HEAD_DIM), BF16),
        compiler_params=pltpu.CompilerParams(
            dimension_semantics=("arbitrary", "arbitrary"), vmem_limit_bytes=VMEM_LIMIT),
    )(x2, pe, w1, w2)


def _nsa_body(qt_ref, sm_ref, kc_ref, vct_ref, ks_ref, vst_ref, kw_ref, vwt_ref, out_ref, *, TQ, TK, S, NCP):
    g = pl.program_id(1)
    t0 = pl.program_id(2) * TQ
    R = A_REP
    M = R * TQ
    D = A_HEAD_DIM
    NS = S // SEL_BLOCK
    n_sel = min(N_SELECT, NS)
    lanes = [slice(r * TQ, (r + 1) * TQ) for r in range(R)]
    dt = lax.broadcasted_iota(jnp.int32, (1, TQ), 1)

    slope = jnp.zeros((1, M), F32)
    head = lax.broadcasted_iota(jnp.int32, (1, M), 1) >> (TQ.bit_length() - 1)
    for r in range(R):
        s_r = jnp.where(g == 0, 2.0 ** (-ALIBI_MAX * (r + 1) / A_HEADS),
                        2.0 ** (-ALIBI_MAX * (R + r + 1) / A_HEADS)).astype(F32)
        slope = jnp.where(head == r, s_r, slope)
    row8 = lax.broadcasted_iota(jnp.int32, (8, M), 0)
    feat = jnp.where(row8 == 0, slope * 256.0, jnp.where(row8 == 1, slope, 0.0))
    q_f = qt_ref[...].astype(F32)
    q_plain = jnp.concatenate([q_f, feat, jnp.zeros((LANES - D - 8, M), F32)], axis=0).astype(BF16)

    CW = max(2 * LANES, TQ)
    chains = [slice(c * CW, (c + 1) * CW) for c in range(M // CW)]

    def attend(k_tile, q_aug, v_tile, mask_t, states):
        cat = lambda xs: jnp.concatenate(xs, axis=1)
        hpc = CW // TQ
        scores = [_dot(k_tile, q_aug[:, ch]) for ch in chains]
        blks = [s_t[:, lanes[r]] for s_t in scores for r in range(hpc)]
        if mask_t is not None:
            blks = [blk + mask_t for blk in blks]
        m_new = [jnp.max(blk, axis=0, keepdims=True) for blk in blks]
        if states is not None:
            m_old = [st[0][:, lanes[r]] for st in states for r in range(hpc)]
            m_new = [jnp.maximum(a, b) for a, b in zip(m_new, m_old)]
            alpha = [jnp.exp(a - b) for a, b in zip(m_old, m_new)]
        probs = [jnp.exp((blk - m).astype(BF16)) for blk, m in zip(blks, m_new)]
        group = lambda xs: [cat(xs[c * hpc:(c + 1) * hpc]) for c in range(len(chains))]
        pvs = [_dot(v_tile, p) for p in group(probs)]
        if states is None:
            return pvs
        return tuple((m, a * st[1] + pv) for m, a, st, pv in zip(group(m_new), group(alpha), states, pvs))

    wk = min(WINDOW + TQ, S)
    k0 = pl.multiple_of(jnp.maximum(t0 + TQ - wk, 0), TQ)
    k_band = kw_ref[pl.ds(k0, wk), :]
    v_band = vwt_ref[pl.ds(k0 // TQ, wk // TQ)]
    v_band = jnp.concatenate([v_band[j] for j in range(wk // TQ)], axis=1)
    dist = dt - (k0 - t0 + lax.broadcasted_iota(jnp.int32, (wk, 1), 0))
    mask_w = jnp.where((dist >= 0) & (dist < WINDOW), 0.0, NEG)
    acc_w = jnp.concatenate(attend(k_band, q_plain, v_band, mask_w, None), axis=1)
    o_w = acc_w[:D] * (1.0 / acc_w[D:D + 1])

    cend = lax.broadcasted_iota(jnp.int32, (NCP, 1), 0) * CMP_STRIDE + (CMP_BLOCK - 1) - t0
    mask_c = jnp.where(cend <= dt, 0.0, NEG)
    s_c = _dot(kc_ref[...], q_plain)
    p_sum = jnp.zeros((NCP, TQ), F32)
    p_c = []
    for r in range(R):
        blk = s_c[:, lanes[r]] + mask_c
        mx = jnp.maximum(jnp.max(blk, axis=0, keepdims=True), 0.1 * NEG)
        e = jnp.exp(blk - mx)
        p = e * (1.0 / jnp.maximum(jnp.sum(e, axis=0, keepdims=True), 1e-30))
        p_sum = p_sum + p
        p_c.append(p.astype(BF16))
    o_c = _dot(vct_ref[...], jnp.concatenate(p_c, axis=1))

    jc = lax.broadcasted_iota(jnp.int32, (NS, 1), 0)
    nr = lax.broadcasted_iota(jnp.int32, (1, NCP), 1)
    overlap_t = ((nr * CMP_STRIDE <= jc * SEL_BLOCK + (SEL_BLOCK - 1))
                 & (nr * CMP_STRIDE + (CMP_BLOCK - 1) >= jc * SEL_BLOCK))
    imp_t = _dot(jnp.where(overlap_t, 1.0, 0.0).astype(BF16), p_sum.astype(BF16))
    cur = (t0 + dt) >> (SEL_BLOCK.bit_length() - 1)
    forced = (jc == 0) | (jc == cur) | (jc == cur - 1)
    score = jnp.where(jc <= cur, jnp.where(forced, BIG, imp_t), NEG)
    sub = lax.broadcasted_iota(jnp.int32, (8, TQ), 0)
    blocks = [score[8 * jb:8 * jb + 8] for jb in range(NS // 8)]
    ranks = [jnp.zeros((8, TQ), F32) for _ in blocks]
    for i in range(NS):
        s_i = score[i:i + 1, :]
        for jb, blk in enumerate(blocks):
            if 8 * jb > i:
                beats = jnp.where(s_i >= blk, 1.0, 0.0)
            elif 8 * jb + 7 < i:
                beats = jnp.where(s_i > blk, 1.0, 0.0)
            else:
                beats = jnp.where(sub + 8 * jb > i, jnp.where(s_i >= blk, 1.0, 0.0), jnp.where(s_i > blk, 1.0, 0.0))
            ranks[jb] = ranks[jb] + beats
    sel_t = jnp.where(jnp.concatenate(ranks, axis=0) < n_sel, 0.0, NEG)
    sel_rows = [jnp.concatenate([sel_t] * R, axis=1)]
    if NS < LANES - D:
        sel_rows.append(jnp.zeros((LANES - D - NS, M), F32))
    q_sel = jnp.concatenate([q_f] + sel_rows + [feat, jnp.zeros((LANES - 8, M), F32)], axis=0).astype(BF16)

    def sel_tile(kt, carry, mask_t=None):
        k_tile = ks_ref[pl.ds(pl.multiple_of(kt * TK, TK), TK), :]
        v_tile = vst_ref[kt]
        return attend(k_tile, q_sel, v_tile, mask_t, carry)

    kd = t0 // TK
    init = tuple((jnp.full((1, CW), NEG, F32), jnp.zeros((LANES, CW), F32)) for _ in chains)
    carry = lax.fori_loop(0, kd, sel_tile, init)
    mask_d = jnp.where(kd * TK - t0 + lax.broadcasted_iota(jnp.int32, (TK, 1), 0) <= dt, 0.0, NEG)
    acc_s = jnp.concatenate([acc for _, acc in sel_tile(kd, carry, mask_d)], axis=1)
    o_s = acc_s[:D] * (1.0 / acc_s[D:D + 1])

    sm_t = sm_ref[...].T
    mixed = []
    for r in range(R):
        gates = []
        for c in range(3):
            c0 = AG_OFF + r * 3 + c
            c1 = AG_OFF + (R + r) * 3 + c
            gates.append(jax.nn.sigmoid(jnp.where(g == 0, sm_t[c0:c0 + 1, :], sm_t[c1:c1 + 1, :])))
        mixed.append(gates[0] * o_c[:, lanes[r]] + gates[1] * o_s[:, lanes[r]] + gates[2] * o_w[:, lanes[r]])
    for r in range(0, R, 2):
        pair = jnp.concatenate([mixed[r], mixed[r + 1]], axis=0).T
        out_ref[:, r * D:(r + 2) * D] = pair.astype(out_ref.dtype)


def _nsa(qt, sm, kc, vct, ks, vst, kw, vwt, batch, seq, TQ, TK):
    n = batch * seq
    G = A_KV_GROUPS
    nq = seq // TQ
    ncp = kc.shape[1]
    per_group = lambda a: pl.BlockSpec((None,) + a.shape[1:], lambda b, g, i: (b * G + g,) + (0,) * (a.ndim - 1))
    return pl.pallas_call(
        functools.partial(_nsa_body, TQ=TQ, TK=TK, S=seq, NCP=ncp),
        grid=(batch, G, nq),
        in_specs=[
            pl.BlockSpec((None, None, None, A_HEAD_DIM, A_REP * TQ), lambda b, g, i: (b, g, i, 0, 0)),
            pl.BlockSpec((TQ, NSM), lambda b, g, i: (b * nq + i, 0)),
            per_group(kc), per_group(vct), per_group(ks), per_group(vst), per_group(kw), per_group(vwt),
        ],
        out_specs=pl.BlockSpec((TQ, A_REP * A_HEAD_DIM), lambda b, g, i: (b * nq + i, g)),
        out_shape=jax.ShapeDtypeStruct((n, A_WIDTH), BF16),
        compiler_params=pltpu.CompilerParams(
            dimension_semantics=("arbitrary", "arbitrary", "arbitrary"), vmem_limit_bytes=VMEM_LIMIT),
    )(qt, sm, kc, vct, ks, vst, kw, vwt)


def _outblock_body(x_ref, mo_ref, ao_ref, p_ref, wout_ref, ln2_ref, wup_ref, cw_ref, cb_ref, wdn_ref,
                   pg_ref, wpg_ref, wpp_ref, fg_ref, out_ref, tail_ref, *, tm, tiles_per_seq, fc):
    @pl.when(pl.program_id(0) % tiles_per_seq == 0)
    def _():
        tail_ref[...] = jnp.zeros_like(tail_ref)

    x1 = (x_ref[...] + _dot(mo_ref[...], wout_ref[:M_WIDTH, :]) + _dot(ao_ref[...], wout_ref[M_WIDTH:, :]))
    h2 = _rms(x1, ln2_ref[...]).astype(BF16)
    rowi = lax.broadcasted_iota(jnp.int32, (tm, 1), 0)
    acc = jnp.zeros(x1.shape, F32)
    for c0 in range(0, D_FF, fc):
        gate = _dot(h2, wup_ref[:, c0:c0 + fc])
        val = _dot(h2, wup_ref[:, D_FF + c0:D_FF + c0 + fc])
        tail = tail_ref[:, c0:c0 + fc]
        g_m1 = jnp.where(rowi == 0, tail[7:8], pltpu.roll(gate, 1, 0))
        g_m2 = jnp.where(rowi == 0, tail[6:7], jnp.where(rowi == 1, tail[7:8], pltpu.roll(gate, 2, 0)))
        conv = (cw_ref[0:1, c0:c0 + fc] * g_m2 + cw_ref[1:2, c0:c0 + fc] * g_m1
                + cw_ref[2:3, c0:c0 + fc] * gate + cb_ref[:, c0:c0 + fc])
        y = jax.nn.silu(conv) * val
        acc = acc + _dot(y.astype(BF16), wdn_ref[c0:c0 + fc, :])
        tail_ref[:, c0:c0 + fc] = gate[tm - 8:tm]
    x2 = x1 + acc
    gate2 = jax.nn.sigmoid(_dot(_rms(x2, pg_ref[...]).astype(BF16), wpg_ref[...]))
    x3 = x2 + gate2 * _dot(p_ref[...].astype(BF16), wpp_ref[...])
    out_ref[...] = _rms(x3, fg_ref[...])


def _outblock(x2d, mo, ao, p2d, wout, ln2, wup, cw, cb, wdn, pg, wpg, wpp, fg, seq, tm):
    n, d = x2d.shape
    row = lambda width: pl.BlockSpec((tm, width), lambda i: (i, 0))
    return pl.pallas_call(
        functools.partial(_outblock_body, tm=tm, tiles_per_seq=seq // tm, fc=512),
        grid=(n // tm,),
        in_specs=[
            row(d), row(M_WIDTH), row(A_WIDTH), row(p2d.shape[1]),
            _resident(wout.shape), _resident((1, d)), _resident(wup.shape), _resident(cw.shape),
            _resident((1, D_FF)), _resident(wdn.shape), _resident((1, d)), _resident(wpg.shape),
            _resident(wpp.shape), _resident((1, d)),
        ],
        out_specs=row(d),
        out_shape=jax.ShapeDtypeStruct((n, d), F32),
        scratch_shapes=[pltpu.VMEM((8, D_FF), F32)],
        compiler_params=pltpu.CompilerParams(
            dimension_semantics=("arbitrary",), vmem_limit_bytes=VMEM_LIMIT),
    )(x2d, mo, ao, p2d, wout, ln2, wup, cw, cb, wdn, pg, wpg, wpp, fg)


def _layer(x2d, p2d, batch, seq, ln1_g, w_in, gate_bias, mnorm_g, pe_k, pe_v, ck_w1, ck_w2, cv_w1, cv_w2,
           w_out, ln2_g, w_up, conv_w, conv_b, w_down, ple_g, w_pg, w_pp, out_g):
    d = x2d.shape[1]
    G = A_KV_GROUPS
    dh = A_HEAD_DIM
    o_mo = 3 * M_WIDTH
    o_mi = o_mo + M_WIDTH
    o_aq = o_mi + 2 * M_HEADS
    o_ag = o_aq + A_WIDTH + 6 * KV_WIDTH
    w = jnp.concatenate([
        w_in[:, :o_mo], w_in[:, o_aq:o_aq + A_WIDTH] * (dh ** -0.5), w_in[:, o_aq + A_WIDTH:o_ag],
        w_in[:, o_mo:o_mi],
        w_in[:, o_mi:o_aq], w_in[:, o_ag:], jnp.zeros((d, NSM - 2 * M_HEADS - 3 * A_HEADS), w_in.dtype),
    ], axis=1).astype(BF16)
    bias = jnp.concatenate([gate_bias, jnp.zeros((NSM - 2 * M_HEADS,), F32)])[None, :]

    L = min(128, seq)
    zmq, zaq, zkv, zo, sm, smt = _inproj(x2d, ln1_g[None, :], w, bias, tm=512, chunk=L)
    m_out = _mlstm(zmq, zo, sm, smt, mnorm_g[None, :], batch, seq, L=L, NBT=2)

    TQ, TK = 256, min(512, seq)
    nq = seq // TQ
    qt = (zaq.reshape(batch, nq, TQ, G, A_REP, dh).transpose(0, 3, 1, 5, 4, 2)
          .reshape(batch, G, nq, dh, A_REP * TQ))
    kv6 = zkv.reshape(batch, seq, 6, G, dh).transpose(2, 0, 3, 1, 4)
    ncp = seq // CMP_STRIDE
    x2 = kv6[0:2].reshape(2, batch * G, ncp, CMP_STRIDE * dh)
    pe = jnp.stack([pe_k, pe_v]).reshape(2, 1, CMP_BLOCK * dh)
    pe = jnp.broadcast_to(pe, (2, 8, CMP_BLOCK * dh)).astype(BF16)
    w1 = jnp.stack([ck_w1, cv_w1]).astype(BF16)
    w2 = jnp.stack([ck_w2, cv_w2]).astype(BF16)
    kvc = _compress(x2, pe, w1, w2)

    bg = batch * G
    kvs = kv6[2:].reshape(4, bg, seq, dh)
    tile = lambda a: jnp.broadcast_to(a.astype(BF16), (bg,) + a.shape)
    pos = jnp.arange(seq)[:, None]
    pos_feat = jnp.concatenate([pos // 256, pos % 256], axis=1)
    onehot = pos // SEL_BLOCK == jnp.arange(LANES - dh)[None, :]
    ks_aug = jnp.concatenate([kvs[0], tile(onehot), tile(pos_feat), jnp.zeros((bg, seq, LANES - 2), BF16)], axis=-1)
    kw_aug = jnp.concatenate([kvs[2], tile(pos_feat), jnp.zeros((bg, seq, LANES - dh - 2), BF16)], axis=-1)
    cpos = jnp.concatenate([jnp.zeros((ncp, 1), jnp.int32), jnp.arange(ncp)[:, None] * CMP_STRIDE], axis=1)
    kc_aug = jnp.concatenate([kvc[0], tile(cpos), jnp.zeros((bg, ncp, LANES - dh - 2), BF16)], axis=-1)
    ones = jnp.zeros((bg, seq, LANES - dh), BF16).at[:, :, 0].set(1.0)
    as_tiles = lambda v, t: (jnp.concatenate([v, ones], axis=-1).reshape(bg, seq // t, t, LANES)
                             .transpose(0, 1, 3, 2))
    a_out = _nsa(qt, sm, kc_aug, kvc[1].transpose(0, 2, 1), ks_aug, as_tiles(kvs[1], TK), kw_aug,
                 as_tiles(kvs[3], TQ), batch, seq, TQ=TQ, TK=TK)

    cw = jnp.concatenate([conv_w, jnp.zeros((8 - conv_w.shape[0], D_FF), F32)], axis=0)
    return _outblock(x2d, m_out, a_out, p2d, w_out.astype(BF16), ln2_g[None, :], w_up.astype(BF16), cw,
                     conv_b[None, :], w_down.astype(BF16), ple_g[None, :], w_pg.astype(BF16),
                     w_pp.astype(BF16), out_g[None, :], seq, tm=256)


def kernel(x, p, ln1_g, w_in, mlstm_gate_bias, mlstm_norm_g, cmp_pos_k, cmp_pos_v, cmp_k_w1, cmp_k_w2,
           cmp_v_w1, cmp_v_w2, w_out, ln2_g, w_up, conv_w, conv_b, w_down, ple_norm_g, w_ple_gate,
           w_ple_proj, final_g):
    batch, seq, d = x.shape
    depth = w_in.shape[0]
    assert depth == 1, "the fused output block applies the final norm, so a single layer is supported"
    assert seq % 128 == 0 and WINDOW + 128 <= seq <= SEL_BLOCK * (LANES - A_HEAD_DIM)
    x2d = x.reshape(batch * seq, d)
    i = 0
    out = _layer(x2d, p[i].reshape(batch * seq, -1), batch, seq, ln1_g[i], w_in[i], mlstm_gate_bias[i],
                 mlstm_norm_g[i], cmp_pos_k[i], cmp_pos_v[i], cmp_k_w1[i], cmp_k_w2[i], cmp_v_w1[i],
                 cmp_v_w2[i], w_out[i], ln2_g[i], w_up[i], conv_w[i], conv_b[i], w_down[i], ple_norm_g[i],
                 w_ple_gate[i], w_ple_proj[i], final_g)
    return out.reshape(batch, seq, d)
```

```python
import functools

import jax
import jax.numpy as jnp
from jax import lax
from jax.experimental import pallas as pl
from jax.experimental.pallas import tpu as pltpu

F32 = jnp.float32
BF16 = jnp.bfloat16

M_HEADS = 4
M_HEAD_DIM = 128
M_WIDTH = M_HEADS * M_HEAD_DIM
A_HEADS = 8
A_KV_GROUPS = 2
A_REP = A_HEADS // A_KV_GROUPS
A_HEAD_DIM = 64
A_WIDTH = A_HEADS * A_HEAD_DIM
KV_WIDTH = A_KV_GROUPS * A_HEAD_DIM
CMP_BLOCK = 32
CMP_STRIDE = 16
CMP_HIDDEN = 128
SEL_BLOCK = 64
N_SELECT = 16
WINDOW = 512
ALIBI_MAX = 8.0
D_FF = 2048
EPS = 1e-6
NEG = -1e30
BIG = 1e30

LANES = 128
VMEM_LIMIT = 56 * 1024 * 1024

NB = 3 * M_WIDTH + A_WIDTH + 6 * KV_WIDTH
NO = M_WIDTH
NSM = LANES
AQ_OFF = 3 * M_WIDTH
KV_OFF = AQ_OFF + A_WIDTH
AG_OFF = 2 * M_HEADS


def _dot(a, b):
    return jnp.dot(a, b, preferred_element_type=F32)


def _dot_nt(a, b):
    return lax.dot_general(a, b, (((1,), (1,)), ((), ())), preferred_element_type=F32)


def _rms(x, g):
    return x * lax.rsqrt(jnp.mean(x * x, axis=-1, keepdims=True) + EPS) * g


def _log_sigmoid(x):
    return jnp.minimum(x, 0.0) - jnp.log1p(jnp.exp(-jnp.abs(x)))


def _resident(shape):
    zeros = (0,) * len(shape)
    return pl.BlockSpec(shape, lambda *_: zeros, pipeline_mode=pl.Buffered(1))


def _inproj_body(x_ref, g_ref, w_ref, bias_ref, tri_ref, zm_ref, zq_ref, zkv_ref, zo_ref, sm_ref, smt_ref):
    h = _rms(x_ref[...], g_ref[...]).astype(BF16)
    sm_t = (_dot(h, w_ref[:, NB + NO:]) + bias_ref[...]).T
    rows = 4 * M_HEADS
    gate = sm_t[:rows]
    row = lax.broadcasted_iota(jnp.int32, gate.shape, 0)
    is_f = (row >= M_HEADS) & (row < 2 * M_HEADS)
    rem = _log_sigmoid(gate)
    parts = []
    for _ in range(3):
        parts.append(rem.astype(BF16))
        rem = rem - parts[-1].astype(F32)

    cw = 512
    for ref, base in ((zm_ref, 0), (zq_ref, AQ_OFF), (zkv_ref, KV_OFF)):
        for c0 in range(0, ref.shape[1], cw):
            c1 = min(c0 + cw, ref.shape[1])
            ref[:, c0:c1] = _dot(h, w_ref[:, base + c0:base + c1]).astype(BF16)
    zo_ref[...] = _dot(h, w_ref[:, NB:NB + NO])

    cum = _dot(parts[0], tri_ref[...]) + _dot(parts[1], tri_ref[...]) + _dot(parts[2], tri_ref[...])
    gate = jnp.where(is_f, cum, gate)
    smt_ref[...] = gate[:2 * M_HEADS]
    sm_ref[...] = jnp.concatenate([gate, sm_t[rows:]], axis=0).T


def _inproj(x2d, g, w, bias, tm, chunk):
    n, d = x2d.shape
    idx = jnp.arange(tm)
    tri = ((idx[:, None] // chunk == idx[None, :] // chunk) & (idx[:, None] <= idx[None, :])).astype(BF16)
    return pl.pallas_call(
        _inproj_body,
        grid=(n // tm,),
        in_specs=[
            pl.BlockSpec((tm, d), lambda i: (i, 0)),
            _resident((1, d)),
            _resident(w.shape),
            _resident((1, NSM)),
            _resident((tm, tm)),
        ],
        out_specs=[
            pl.BlockSpec((tm, AQ_OFF), lambda i: (i, 0)),
            pl.BlockSpec((tm, KV_OFF - AQ_OFF), lambda i: (i, 0)),
            pl.BlockSpec((tm, NB - KV_OFF), lambda i: (i, 0)),
            pl.BlockSpec((tm, NO), lambda i: (i, 0)),
            pl.BlockSpec((tm, NSM), lambda i: (i, 0)),
            pl.BlockSpec((2 * M_HEADS, tm), lambda i: (0, i)),
        ],
        out_shape=[
            jax.ShapeDtypeStruct((n, AQ_OFF), BF16),
            jax.ShapeDtypeStruct((n, KV_OFF - AQ_OFF), BF16),
            jax.ShapeDtypeStruct((n, NB - KV_OFF), BF16),
            jax.ShapeDtypeStruct((n, NO), F32),
            jax.ShapeDtypeStruct((n, NSM), F32),
            jax.ShapeDtypeStruct((2 * M_HEADS, n), F32),
        ],
        compiler_params=pltpu.CompilerParams(
            dimension_semantics=("arbitrary",), vmem_limit_bytes=VMEM_LIMIT),
    )(x2d, g, w, bias, tri)


def _mlstm_body(zb_ref, zo_ref, sm_ref, *rest, L, NBT):
    smt_refs, (gn_ref, out_ref, c_ref, m_ref) = rest[:NBT], rest[NBT:]

    @pl.when(pl.program_id(1) == 0)
    def _():
        c_ref[...] = jnp.zeros_like(c_ref)
        m_ref[...] = jnp.zeros_like(m_ref)

    dh = M_HEAD_DIM
    scale = dh ** -0.5
    causal = lax.broadcasted_iota(jnp.int32, (L, L), 1) <= lax.broadcasted_iota(jnp.int32, (L, L), 0)
    chains = [(bi, h) for bi in range(NBT) for h in range(M_HEADS)]
    each = lambda f: [f(i, bi, h) for i, (bi, h) in enumerate(chains)]
    col = lambda h: slice(h * dh, (h + 1) * dh)
    ones = (lax.broadcasted_iota(jnp.int32, (L, LANES), 1) == 0).astype(BF16)

    q = each(lambda i, bi, h: zb_ref[bi, :, col(h)])
    k = each(lambda i, bi, h: zb_ref[bi, :, M_WIDTH + h * dh:M_WIDTH + (h + 1) * dh])
    v = each(lambda i, bi, h: jnp.concatenate([zb_ref[bi, :, 2 * M_WIDTH + h * dh:2 * M_WIDTH + (h + 1) * dh],
                                               ones], axis=1))
    c_prev = each(lambda i, bi, h: c_ref[i])
    qk = each(lambda i, bi, h: _dot_nt(q[i], k[i]))
    qc = each(lambda i, bi, h: _dot(q[i], c_prev[i].astype(BF16)))

    ig_col = each(lambda i, bi, h: sm_ref[bi, :, h:h + 1])
    b_col = each(lambda i, bi, h: sm_ref[bi, :, M_HEADS + h:M_HEADS + h + 1])
    ig_row = each(lambda i, bi, h: smt_refs[bi][h:h + 1, :])
    b_row = each(lambda i, bi, h: smt_refs[bi][M_HEADS + h:M_HEADS + h + 1, :])
    b_last = each(lambda i, bi, h: b_row[i][:, L - 1:L])
    m_prev = each(lambda i, bi, h: m_ref[i][:, :1])

    dm = each(lambda i, bi, h: jnp.where(causal, b_col[i] - b_row[i] + ig_row[i], -jnp.inf))
    bq = each(lambda i, bi, h: b_col[i] + m_prev[i])
    dm_max = each(lambda i, bi, h: jnp.max(dm[i], axis=1, keepdims=True))
    m_loc = each(lambda i, bi, h: jnp.maximum(bq[i], dm_max[i]))
    inter = each(lambda i, bi, h: jnp.exp(bq[i] - m_loc[i]))
    wt = each(lambda i, bi, h: (jnp.exp(dm[i] - m_loc[i]) * qk[i]).astype(BF16))

    a_row = each(lambda i, bi, h: b_last[i] - b_row[i] + ig_row[i])
    a_max = each(lambda i, bi, h: jnp.max(a_row[i], axis=1, keepdims=True))
    m_new = each(lambda i, bi, h: jnp.maximum(b_last[i] + m_prev[i], a_max[i]))
    decay = each(lambda i, bi, h: jnp.exp(b_last[i] + m_prev[i] - m_new[i]))
    k_t = each(lambda i, bi, h: k[i].astype(F32).T)
    kw_t = each(lambda i, bi, h: (k_t[i] * jnp.exp(a_row[i] - m_new[i])).astype(BF16))

    wv = each(lambda i, bi, h: _dot(wt[i], v[i]))
    kv = each(lambda i, bi, h: _dot(kw_t[i], v[i]))
    for i in range(len(chains)):
        c_ref[i] = decay[i] * c_prev[i] + kv[i]
        m_ref[i] = jnp.broadcast_to(m_new[i], (1, LANES))

    tot = each(lambda i, bi, h: inter[i] * qc[i] + wv[i])
    floor = each(lambda i, bi, h: jnp.exp(-m_loc[i]))
    hh = each(lambda i, bi, h: (tot[i][:, :dh] * scale)
              / jnp.maximum(jnp.abs(tot[i][:, dh:dh + 1]) * scale, floor[i]))
    ms = each(lambda i, bi, h: jnp.mean(hh[i] * hh[i], axis=-1, keepdims=True))
    for i, (bi, h) in enumerate(chains):
        hn = hh[i] * lax.rsqrt(ms[i] + EPS) * gn_ref[:, col(h)]
        out_ref[bi, :, col(h)] = (hn * jax.nn.sigmoid(zo_ref[bi, :, col(h)])).astype(out_ref.dtype)


def _mlstm(zb, zo, sm, smt, gn, batch, seq, L, NBT):
    nc = seq // L
    rows = lambda width: pl.BlockSpec((NBT, L, width), lambda b, c: (b, c, 0))
    smt_specs = [pl.BlockSpec((2 * M_HEADS, L), functools.partial(lambda b, c, j: (0, (b * NBT + j) * nc + c), j=j))
                 for j in range(NBT)]
    nchain = NBT * M_HEADS
    out = pl.pallas_call(
        functools.partial(_mlstm_body, L=L, NBT=NBT),
        grid=(batch // NBT, nc),
        in_specs=[rows(zb.shape[-1]), rows(NO), rows(NSM)] + smt_specs + [_resident((1, M_WIDTH))],
        out_specs=rows(M_WIDTH),
        out_shape=jax.ShapeDtypeStruct((batch, seq, M_WIDTH), BF16),
        scratch_shapes=[
            pltpu.VMEM((nchain, M_HEAD_DIM, 2 * M_HEAD_DIM), F32),
            pltpu.VMEM((nchain, 1, LANES), F32),
        ],
        compiler_params=pltpu.CompilerParams(
            dimension_semantics=("arbitrary", "arbitrary"), vmem_limit_bytes=VMEM_LIMIT),
    )(zb.reshape(batch, seq, -1), zo.reshape(batch, seq, -1), sm.reshape(batch, seq, -1), *([smt] * NBT), gn)
    return out.reshape(batch * seq, M_WIDTH)


def _compress_body(x_ref, pe_ref, w1_ref, w2_ref, out_ref, *, ncp):
    half = (CMP_BLOCK // 2) * A_HEAD_DIM
    x = x_ref[...]
    first = _dot(x, w1_ref[:half, :])
    second = _dot(x, w1_ref[half:, :])
    second = pltpu.roll(second, ncp - 1, 0)
    pe_term = _dot(pe_ref[...], w1_ref[...])[0:1]
    hid = jax.nn.gelu(first + second + pe_term)
    out_ref[...] = _dot(hid.astype(BF16), w2_ref[...]).astype(out_ref.dtype)


def _compress(x2, pe, w1, w2):
    two, bg, ncp, width = x2.shape
    return pl.pallas_call(
        functools.partial(_compress_body, ncp=ncp),
        grid=(two, bg),
        in_specs=[
            pl.BlockSpec((None, None, ncp, width), lambda a, i: (a, i, 0, 0)),
            pl.BlockSpec((None, 8, CMP_BLOCK * A_HEAD_DIM), lambda a, i: (a, 0, 0)),
            pl.BlockSpec((None, CMP_BLOCK * A_HEAD_DIM, CMP_HIDDEN), lambda a, i: (a, 0, 0)),
            pl.BlockSpec((None, CMP_HIDDEN, A_HEAD_DIM), lambda a, i: (a, 0, 0)),
        ],
        out_specs=pl.BlockSpec((None, None, ncp, A_HEAD_DIM), lambda a, i: (a, i, 0, 0)),
        out_shape=jax.ShapeDtypeStruct((two, bg, ncp, A_HEAD_DIM), BF16),
        compiler_params=pltpu.CompilerParams(
            dimension_semantics=("arbitrary", "arbitrary"), vmem_limit_bytes=VMEM_LIMIT),
    )(x2, pe, w1, w2)


V_ROWS = LANES + 16


def _nsa_body(zq_ref, sm_ref, kc_ref, vct_ref, ks_ref, vs_ref, kw_ref, vw_ref, kconst_ref, out_ref,
              vst_ref, vwt_ref, *, TQ, TK, S, NCP):
    g = pl.program_id(1)
    t0 = pl.program_id(2) * TQ
    R = A_REP
    M = R * TQ
    D = A_HEAD_DIM
    NS = S // SEL_BLOCK
    n_sel = min(N_SELECT, NS)
    lanes = [slice(r * TQ, (r + 1) * TQ) for r in range(R)]
    dt = lax.broadcasted_iota(jnp.int32, (1, TQ), 1)

    @pl.when((g == 0) & (pl.program_id(2) == 0))
    def _():
        ones_rows = (lax.broadcasted_iota(jnp.int32, (V_ROWS - LANES, 1), 0) == 0).astype(F32)
        for v_ref, vt_ref in ((vs_ref, vst_ref), (vw_ref, vwt_ref)):
            t = vt_ref.shape[2]
            for j in range(S // t):
                v_t = v_ref[j * t:(j + 1) * t, :].astype(F32).T
                vt_ref[j] = jnp.concatenate([v_t, jnp.broadcast_to(ones_rows, (V_ROWS - LANES, t))],
                                            axis=0).astype(BF16)

    slope = jnp.zeros((1, M), F32)
    head = lax.broadcasted_iota(jnp.int32, (1, M), 1) >> (TQ.bit_length() - 1)
    for r in range(R):
        s_r = jnp.where(g == 0, 2.0 ** (-ALIBI_MAX * (r + 1) / A_HEADS),
                        2.0 ** (-ALIBI_MAX * (R + r + 1) / A_HEADS)).astype(F32)
        slope = jnp.where(head == r, s_r, slope)
    row8 = lax.broadcasted_iota(jnp.int32, (8, M), 0)
    feat = jnp.where(row8 == 0, slope * 256.0, jnp.where(row8 == 1, slope, 0.0))
    q_t = zq_ref[...].astype(F32).T
    q_f = jnp.concatenate([q_t[r * D:(r + 1) * D] for r in range(R)], axis=1)
    zeros = lambda n: jnp.zeros((n, M), F32)
    q_cmp = jnp.concatenate([q_f, feat, zeros(LANES - D - 8)], axis=0).astype(BF16)
    q_grp = jnp.where(g == 0, jnp.concatenate([q_f, zeros(D)], axis=0),
                      jnp.concatenate([zeros(D), q_f], axis=0))

    CW = max(2 * LANES, TQ)
    chains = [slice(c * CW, (c + 1) * CW) for c in range(M // CW)]

    def attend(k_tile, q_aug, v_tile, mask_t, states):
        cat = lambda xs: jnp.concatenate(xs, axis=1)
        hpc = CW // TQ
        scores = [_dot(k_tile, q_aug[:, ch]) for ch in chains]
        blks = [s_t[:, lanes[r]] for s_t in scores for r in range(hpc)]
        if mask_t is not None:
            blks = [blk + mask_t for blk in blks]
        m_new = [jnp.max(blk, axis=0, keepdims=True) for blk in blks]
        if states is not None:
            m_old = [st[0][:, lanes[r]] for st in states for r in range(hpc)]
            m_new = [jnp.maximum(a, b) for a, b in zip(m_new, m_old)]
            alpha = [jnp.exp(a - b) for a, b in zip(m_old, m_new)]
        probs = [jnp.exp((blk - m).astype(BF16)) for blk, m in zip(blks, m_new)]
        group = lambda xs: [cat(xs[c * hpc:(c + 1) * hpc]) for c in range(len(chains))]
        pvs = [_dot(v_tile, p) for p in group(probs)]
        if states is None:
            return pvs
        return tuple((m, a * st[1] + pv) for m, a, st, pv in zip(group(m_new), group(alpha), states, pvs))

    def normalise(accs):
        acc = jnp.concatenate(accs, axis=1)
        return jnp.where(g == 0, acc[:D], acc[D:2 * D]) * (1.0 / acc[LANES:LANES + 1])

    q_plain = jnp.concatenate([q_grp, zeros(LANES - D), feat, zeros(LANES - D - 8)], axis=0).astype(BF16)
    wk = min(WINDOW + TQ, S)
    k0 = pl.multiple_of(jnp.maximum(t0 + TQ - wk, 0), TQ)
    k_band = jnp.concatenate([kw_ref[pl.ds(k0, wk), :], kconst_ref[pl.ds(k0, wk), :]], axis=1)
    v_band = vwt_ref[pl.ds(k0 // TQ, wk // TQ)]
    v_band = jnp.concatenate([v_band[j] for j in range(wk // TQ)], axis=1)
    dist = dt - (k0 - t0 + lax.broadcasted_iota(jnp.int32, (wk, 1), 0))
    mask_w = jnp.where((dist >= 0) & (dist < WINDOW), 0.0, NEG)
    o_w = normalise(attend(k_band, q_plain, v_band, mask_w, None))

    cend = lax.broadcasted_iota(jnp.int32, (NCP, 1), 0) * CMP_STRIDE + (CMP_BLOCK - 1) - t0
    mask_c = jnp.where(cend <= dt, 0.0, NEG)
    s_c = _dot(kc_ref[...], q_cmp)
    p_sum = jnp.zeros((NCP, TQ), F32)
    p_c = []
    for r in range(R):
        blk = s_c[:, lanes[r]] + mask_c
        mx = jnp.maximum(jnp.max(blk, axis=0, keepdims=True), 0.1 * NEG)
        e = jnp.exp(blk - mx)
        p = e * (1.0 / jnp.maximum(jnp.sum(e, axis=0, keepdims=True), 1e-30))
        p_sum = p_sum + p
        p_c.append(p.astype(BF16))
    o_c = _dot(vct_ref[...], jnp.concatenate(p_c, axis=1))

    jc = lax.broadcasted_iota(jnp.int32, (NS, 1), 0)
    nr = lax.broadcasted_iota(jnp.int32, (1, NCP), 1)
    overlap_t = ((nr * CMP_STRIDE <= jc * SEL_BLOCK + (SEL_BLOCK - 1))
                 & (nr * CMP_STRIDE + (CMP_BLOCK - 1) >= jc * SEL_BLOCK))
    imp_t = _dot(jnp.where(overlap_t, 1.0, 0.0).astype(BF16), p_sum.astype(BF16))
    cur = (t0 + dt) >> (SEL_BLOCK.bit_length() - 1)
    forced = (jc == 0) | (jc == cur) | (jc == cur - 1)
    score = jnp.where(jc <= cur, jnp.where(forced, BIG, imp_t), NEG)
    sub = lax.broadcasted_iota(jnp.int32, (8, TQ), 0)
    blocks = [score[8 * jb:8 * jb + 8] for jb in range(NS // 8)]
    ranks = [jnp.zeros((8, TQ), F32) for _ in blocks]
    for i in range(NS):
        s_i = score[i:i + 1, :]
        for jb, blk in enumerate(blocks):
            if 8 * jb > i:
                beats = jnp.where(s_i >= blk, 1.0, 0.0)
            elif 8 * jb + 7 < i:
                beats = jnp.where(s_i > blk, 1.0, 0.0)
            else:
                beats = jnp.where(sub + 8 * jb > i, jnp.where(s_i >= blk, 1.0, 0.0), jnp.where(s_i > blk, 1.0, 0.0))
            ranks[jb] = ranks[jb] + beats
    sel_t = jnp.where(jnp.concatenate(ranks, axis=0) < n_sel, 0.0, NEG)
    sel_rows = [jnp.concatenate([sel_t] * R, axis=1)]
    if NS < LANES - D:
        sel_rows.append(zeros(LANES - D - NS))
    q_sel = jnp.concatenate([q_grp] + sel_rows + [feat, zeros(LANES - D - 8)], axis=0).astype(BF16)

    def sel_tile(kt, carry, mask_t=None):
        rows = pl.ds(pl.multiple_of(kt * TK, TK), TK)
        k_tile = jnp.concatenate([ks_ref[rows, :], kconst_ref[rows, :]], axis=1)
        return attend(k_tile, q_sel, vst_ref[kt], mask_t, carry)

    kd = t0 // TK
    init = tuple((jnp.full((1, CW), NEG, F32), jnp.zeros((V_ROWS, CW), F32)) for _ in chains)
    carry = lax.fori_loop(0, kd, sel_tile, init)
    mask_d = jnp.where(kd * TK - t0 + lax.broadcasted_iota(jnp.int32, (TK, 1), 0) <= dt, 0.0, NEG)
    o_s = normalise([acc for _, acc in sel_tile(kd, carry, mask_d)])

    sm_t = sm_ref[...].T
    mixed = []
    for r in range(R):
        gates = []
        for c in range(3):
            c0 = AG_OFF + r * 3 + c
            c1 = AG_OFF + (R + r) * 3 + c
            gates.append(jax.nn.sigmoid(jnp.where(g == 0, sm_t[c0:c0 + 1, :], sm_t[c1:c1 + 1, :])))
        mixed.append(gates[0] * o_c[:, lanes[r]] + gates[1] * o_s[:, lanes[r]] + gates[2] * o_w[:, lanes[r]])
    for r in range(0, R, 2):
        pair = jnp.concatenate([mixed[r], mixed[r + 1]], axis=0).T
        out_ref[:, r * D:(r + 2) * D] = pair.astype(out_ref.dtype)


def _nsa(zaq, sm, kc, vct, zkv, kconst, batch, seq, TQ, TK):
    n = batch * seq
    G = A_KV_GROUPS
    nq = seq // TQ
    ncp = kc.shape[1]
    per_group = lambda a: pl.BlockSpec((None,) + a.shape[1:], lambda b, g, i: (b * G + g,) + (0,) * (a.ndim - 1))
    kv_block = lambda j: pl.BlockSpec((seq, KV_WIDTH), functools.partial(lambda b, g, i, j: (b, j), j=j))
    return pl.pallas_call(
        functools.partial(_nsa_body, TQ=TQ, TK=TK, S=seq, NCP=ncp),
        grid=(batch, G, nq),
        in_specs=[
            pl.BlockSpec((TQ, A_REP * A_HEAD_DIM), lambda b, g, i: (b * nq + i, g)),
            pl.BlockSpec((TQ, NSM), lambda b, g, i: (b * nq + i, 0)),
            per_group(kc), per_group(vct),
            kv_block(2), kv_block(3), kv_block(4), kv_block(5),
            _resident(kconst.shape),
        ],
        out_specs=pl.BlockSpec((TQ, A_REP * A_HEAD_DIM), lambda b, g, i: (b * nq + i, g)),
        out_shape=jax.ShapeDtypeStruct((n, A_WIDTH), BF16),
        scratch_shapes=[pltpu.VMEM((seq // TK, V_ROWS, TK), BF16), pltpu.VMEM((seq // TQ, V_ROWS, TQ), BF16)],
        compiler_params=pltpu.CompilerParams(
            dimension_semantics=("arbitrary", "arbitrary", "arbitrary"), vmem_limit_bytes=VMEM_LIMIT),
    )(zaq, sm, kc, vct, zkv, zkv, zkv, zkv, kconst)


def _outblock_body(x_ref, mo_ref, ao_ref, p_ref, wout_ref, ln2_ref, wup_ref, cw_ref, cb_ref, wdn_ref,
                   pg_ref, wpg_ref, wpp_ref, fg_ref, out_ref, tail_ref, *, tm, tiles_per_seq, fc):
    @pl.when(pl.program_id(0) % tiles_per_seq == 0)
    def _():
        tail_ref[...] = jnp.zeros_like(tail_ref)

    x1 = (x_ref[...] + _dot(mo_ref[...], wout_ref[:M_WIDTH, :]) + _dot(ao_ref[...], wout_ref[M_WIDTH:, :]))
    ple = _dot(p_ref[...].astype(BF16), wpp_ref[...])
    h2 = _rms(x1, ln2_ref[...]).astype(BF16)
    rowi = lax.broadcasted_iota(jnp.int32, (tm, 1), 0)
    acc = jnp.zeros(x1.shape, F32)
    up = lambda c0: (_dot(h2, wup_ref[:, c0:c0 + fc]), _dot(h2, wup_ref[:, D_FF + c0:D_FF + c0 + fc]))
    nxt = up(0)
    for c0 in range(0, D_FF, fc):
        gate, val = nxt
        if c0 + fc < D_FF:
            nxt = up(c0 + fc)
        tail = tail_ref[:, c0:c0 + fc]
        g_m1 = jnp.where(rowi == 0, tail[7:8], pltpu.roll(gate, 1, 0))
        g_m2 = jnp.where(rowi == 0, tail[6:7], jnp.where(rowi == 1, tail[7:8], pltpu.roll(gate, 2, 0)))
        conv = (cw_ref[0:1, c0:c0 + fc] * g_m2 + cw_ref[1:2, c0:c0 + fc] * g_m1
                + cw_ref[2:3, c0:c0 + fc] * gate + cb_ref[:, c0:c0 + fc])
        y = jax.nn.silu(conv) * val
        acc = acc + _dot(y.astype(BF16), wdn_ref[c0:c0 + fc, :])
        tail_ref[:, c0:c0 + fc] = gate[tm - 8:tm]
    x2 = x1 + acc
    gate2 = jax.nn.sigmoid(_dot(_rms(x2, pg_ref[...]).astype(BF16), wpg_ref[...]))
    x3 = x2 + gate2 * ple
    out_ref[...] = _rms(x3, fg_ref[...])


def _outblock(x2d, mo, ao, p2d, wout, ln2, wup, cw, cb, wdn, pg, wpg, wpp, fg, seq, tm):
    n, d = x2d.shape
    row = lambda width: pl.BlockSpec((tm, width), lambda i: (i, 0))
    return pl.pallas_call(
        functools.partial(_outblock_body, tm=tm, tiles_per_seq=seq // tm, fc=512),
        grid=(n // tm,),
        in_specs=[
            row(d), row(M_WIDTH), row(A_WIDTH), row(p2d.shape[1]),
            _resident(wout.shape), _resident((1, d)), _resident(wup.shape), _resident(cw.shape),
            _resident((1, D_FF)), _resident(wdn.shape), _resident((1, d)), _resident(wpg.shape),
            _resident(wpp.shape), _resident((1, d)),
        ],
        out_specs=row(d),
        out_shape=jax.ShapeDtypeStruct((n, d), F32),
        scratch_shapes=[pltpu.VMEM((8, D_FF), F32)],
        compiler_params=pltpu.CompilerParams(
            dimension_semantics=("arbitrary",), vmem_limit_bytes=VMEM_LIMIT),
    )(x2d, mo, ao, p2d, wout, ln2, wup, cw, cb, wdn, pg, wpg, wpp, fg)


def _layer(x2d, p2d, batch, seq, ln1_g, w_in, gate_bias, mnorm_g, pe_k, pe_v, ck_w1, ck_w2, cv_w1, cv_w2,
           w_out, ln2_g, w_up, conv_w, conv_b, w_down, ple_g, w_pg, w_pp, out_g):
    d = x2d.shape[1]
    G = A_KV_GROUPS
    dh = A_HEAD_DIM
    o_mo = 3 * M_WIDTH
    o_mi = o_mo + M_WIDTH
    o_aq = o_mi + 2 * M_HEADS
    o_ag = o_aq + A_WIDTH + 6 * KV_WIDTH
    w = jnp.concatenate([
        w_in[:, :o_mo], w_in[:, o_aq:o_aq + A_WIDTH] * (dh ** -0.5), w_in[:, o_aq + A_WIDTH:o_ag],
        w_in[:, o_mo:o_mi],
        w_in[:, o_mi:o_aq], w_in[:, o_ag:], jnp.zeros((d, NSM - 2 * M_HEADS - 3 * A_HEADS), w_in.dtype),
    ], axis=1).astype(BF16)
    bias = jnp.concatenate([gate_bias, jnp.zeros((NSM - 2 * M_HEADS,), F32)])[None, :]

    L = min(128, seq)
    zmq, zaq, zkv, zo, sm, smt = _inproj(x2d, ln1_g[None, :], w, bias, tm=512, chunk=L)
    m_out = _mlstm(zmq, zo, sm, smt, mnorm_g[None, :], batch, seq, L=L, NBT=2 if batch % 2 == 0 else 1)

    TQ, TK = 256, min(512, seq)
    ncp = seq // CMP_STRIDE
    bg = batch * G
    x2 = (zkv[:, :2 * KV_WIDTH].reshape(batch, ncp, CMP_STRIDE, 2, G, dh).transpose(3, 0, 4, 1, 2, 5)
          .reshape(2, bg, ncp, CMP_STRIDE * dh))
    pe = jnp.stack([pe_k, pe_v]).reshape(2, 1, CMP_BLOCK * dh)
    pe = jnp.broadcast_to(pe, (2, 8, CMP_BLOCK * dh)).astype(BF16)
    w1 = jnp.stack([ck_w1, cv_w1]).astype(BF16)
    w2 = jnp.stack([ck_w2, cv_w2]).astype(BF16)
    kvc = _compress(x2, pe, w1, w2)

    pos = jnp.arange(seq)[:, None]
    onehot = pos // SEL_BLOCK == jnp.arange(LANES - dh)[None, :]
    kconst = jnp.concatenate([onehot, pos // 256, pos % 256, jnp.zeros((seq, LANES - dh - 2), jnp.int32)],
                             axis=1).astype(BF16)
    cpos = jnp.concatenate([jnp.zeros((ncp, 1), jnp.int32), jnp.arange(ncp)[:, None] * CMP_STRIDE], axis=1)
    kc_aug = jnp.concatenate([kvc[0], jnp.broadcast_to(cpos.astype(BF16), (bg, ncp, 2)),
                              jnp.zeros((bg, ncp, LANES - dh - 2), BF16)], axis=-1)
    a_out = _nsa(zaq, sm, kc_aug, kvc[1].transpose(0, 2, 1), zkv, kconst, batch, seq, TQ=TQ, TK=TK)

    cw = jnp.concatenate([conv_w, jnp.zeros((8 - conv_w.shape[0], D_FF), F32)], axis=0)
    return _outblock(x2d, m_out, a_out, p2d, w_out.astype(BF16), ln2_g[None, :], w_up.astype(BF16), cw,
                     conv_b[None, :], w_down.astype(BF16), ple_g[None, :], w_pg.astype(BF16),
                     w_pp.astype(BF16), out_g[None, :], seq, tm=512)


def kernel(x, p, ln1_g, w_in, mlstm_gate_bias, mlstm_norm_g, cmp_pos_k, cmp_pos_v, cmp_k_w1, cmp_k_w2,
           cmp_v_w1, cmp_v_w2, w_out, ln2_g, w_up, conv_w, conv_b, w_down, ple_norm_g, w_ple_gate,
           w_ple_proj, final_g):
    batch, seq, d = x.shape
    depth = w_in.shape[0]
    assert depth == 1, "the fused output block applies the final norm, so a single layer is supported"
    assert seq % 128 == 0 and WINDOW + 128 <= seq <= SEL_BLOCK * (LANES - A_HEAD_DIM)
    x2d = x.reshape(batch * seq, d)
    i = 0
    out = _layer(x2d, p[i].reshape(batch * seq, -1), batch, seq, ln1_g[i], w_in[i], mlstm_gate_bias[i],
                 mlstm_norm_g[i], cmp_pos_k[i], cmp_pos_v[i], cmp_k_w1[i], cmp_k_w2[i], cmp_v_w1[i],
                 cmp_v_w2[i], w_out[i], ln2_g[i], w_up[i], conv_w[i], conv_b[i], w_down[i], ple_norm_g[i],
                 w_ple_gate[i], w_ple_proj[i], final_g)
    return out.reshape(batch, seq, d)
```

```python
import functools

import jax
import jax.numpy as jnp
from jax import lax
from jax.experimental import pallas as pl
from jax.experimental.pallas import tpu as pltpu

F32 = jnp.float32
BF16 = jnp.bfloat16

M_HEADS = 4
M_HEAD_DIM = 128
M_WIDTH = M_HEADS * M_HEAD_DIM
A_HEADS = 8
A_KV_GROUPS = 2
A_REP = A_HEADS // A_KV_GROUPS
A_HEAD_DIM = 64
A_WIDTH = A_HEADS * A_HEAD_DIM
KV_WIDTH = A_KV_GROUPS * A_HEAD_DIM
CMP_BLOCK = 32
CMP_STRIDE = 16
CMP_HIDDEN = 128
SEL_BLOCK = 64
N_SELECT = 16
WINDOW = 512
ALIBI_MAX = 8.0
D_FF = 2048
EPS = 1e-6
NEG = -1e30
BIG = 1e30

LANES = 128
VMEM_LIMIT = 56 * 1024 * 1024

NB = 3 * M_WIDTH + A_WIDTH + 6 * KV_WIDTH
NO = M_WIDTH
NSM = LANES
AQ_OFF = 3 * M_WIDTH
KV_OFF = AQ_OFF + A_WIDTH
AG_OFF = 2 * M_HEADS


def _dot(a, b):
    return jnp.dot(a, b, preferred_element_type=F32)


def _dot_nt(a, b):
    return lax.dot_general(a, b, (((1,), (1,)), ((), ())), preferred_element_type=F32)


def _rms(x, g):
    return x * lax.rsqrt(jnp.mean(x * x, axis=-1, keepdims=True) + EPS) * g


def _log_sigmoid(x):
    return jnp.minimum(x, 0.0) - jnp.log1p(jnp.exp(-jnp.abs(x)))


def _resident(shape):
    zeros = (0,) * len(shape)
    return pl.BlockSpec(shape, lambda *_: zeros, pipeline_mode=pl.Buffered(1))


def _inproj_body(x_ref, g_ref, w_ref, bias_ref, tri_ref, zm_ref, zq_ref, zkc_ref, zvc_ref, zkv_ref, zo_ref,
                 sm_ref, smt_ref):
    h = _rms(x_ref[...], g_ref[...]).astype(BF16)
    sm_t = (_dot(h, w_ref[:, NB + NO:]) + bias_ref[...]).T
    rows = 4 * M_HEADS
    gate = sm_t[:rows]
    row = lax.broadcasted_iota(jnp.int32, gate.shape, 0)
    is_f = (row >= M_HEADS) & (row < 2 * M_HEADS)
    rem = _log_sigmoid(gate)
    parts = []
    for _ in range(3):
        parts.append(rem.astype(BF16))
        rem = rem - parts[-1].astype(F32)

    cw = 512
    for ref, base in ((zm_ref, 0), (zq_ref, AQ_OFF), (zkc_ref, KV_OFF), (zvc_ref, KV_OFF + KV_WIDTH),
                      (zkv_ref, KV_OFF + 2 * KV_WIDTH)):
        for c0 in range(0, ref.shape[1], cw):
            c1 = min(c0 + cw, ref.shape[1])
            ref[:, c0:c1] = _dot(h, w_ref[:, base + c0:base + c1]).astype(BF16)
    zo_ref[...] = _dot(h, w_ref[:, NB:NB + NO])

    cum = _dot(parts[0], tri_ref[...]) + _dot(parts[1], tri_ref[...]) + _dot(parts[2], tri_ref[...])
    gate = jnp.where(is_f, cum, gate)
    smt_ref[...] = gate[:2 * M_HEADS]
    sm_ref[...] = jnp.concatenate([gate, sm_t[rows:]], axis=0).T


def _inproj(x2d, g, w, bias, tm, chunk):
    n, d = x2d.shape
    idx = jnp.arange(tm)
    tri = ((idx[:, None] // chunk == idx[None, :] // chunk) & (idx[:, None] <= idx[None, :])).astype(BF16)
    return pl.pallas_call(
        _inproj_body,
        grid=(n // tm,),
        in_specs=[
            pl.BlockSpec((tm, d), lambda i: (i, 0)),
            _resident((1, d)),
            _resident(w.shape),
            _resident((1, NSM)),
            _resident((tm, tm)),
        ],
        out_specs=[
            pl.BlockSpec((tm, AQ_OFF), lambda i: (i, 0)),
            pl.BlockSpec((tm, KV_OFF - AQ_OFF), lambda i: (i, 0)),
            pl.BlockSpec((tm, KV_WIDTH), lambda i: (i, 0)),
            pl.BlockSpec((tm, KV_WIDTH), lambda i: (i, 0)),
            pl.BlockSpec((tm, NB - KV_OFF - 2 * KV_WIDTH), lambda i: (i, 0)),
            pl.BlockSpec((tm, NO), lambda i: (i, 0)),
            pl.BlockSpec((tm, NSM), lambda i: (i, 0)),
            pl.BlockSpec((2 * M_HEADS, tm), lambda i: (0, i)),
        ],
        out_shape=[
            jax.ShapeDtypeStruct((n, AQ_OFF), BF16),
            jax.ShapeDtypeStruct((n, KV_OFF - AQ_OFF), BF16),
            jax.ShapeDtypeStruct((n, KV_WIDTH), BF16),
            jax.ShapeDtypeStruct((n, KV_WIDTH), BF16),
            jax.ShapeDtypeStruct((n, NB - KV_OFF - 2 * KV_WIDTH), BF16),
            jax.ShapeDtypeStruct((n, NO), F32),
            jax.ShapeDtypeStruct((n, NSM), F32),
            jax.ShapeDtypeStruct((2 * M_HEADS, n), F32),
        ],
        compiler_params=pltpu.CompilerParams(
            dimension_semantics=("arbitrary",), vmem_limit_bytes=VMEM_LIMIT),
    )(x2d, g, w, bias, tri)


def _mlstm_body(zb_ref, zo_ref, sm_ref, *rest, L, NBT):
    smt_refs, (gn_ref, out_ref, c_ref, m_ref) = rest[:NBT], rest[NBT:]

    @pl.when(pl.program_id(1) == 0)
    def _():
        c_ref[...] = jnp.zeros_like(c_ref)
        m_ref[...] = jnp.zeros_like(m_ref)

    dh = M_HEAD_DIM
    scale = dh ** -0.5
    causal = lax.broadcasted_iota(jnp.int32, (L, L), 1) <= lax.broadcasted_iota(jnp.int32, (L, L), 0)
    chains = [(bi, h) for bi in range(NBT) for h in range(M_HEADS)]
    each = lambda f: [f(i, bi, h) for i, (bi, h) in enumerate(chains)]
    col = lambda h: slice(h * dh, (h + 1) * dh)
    ones = (lax.broadcasted_iota(jnp.int32, (L, LANES), 1) == 0).astype(BF16)

    q = each(lambda i, bi, h: zb_ref[bi, :, col(h)])
    k = each(lambda i, bi, h: zb_ref[bi, :, M_WIDTH + h * dh:M_WIDTH + (h + 1) * dh])
    v = each(lambda i, bi, h: jnp.concatenate([zb_ref[bi, :, 2 * M_WIDTH + h * dh:2 * M_WIDTH + (h + 1) * dh],
                                               ones], axis=1))
    c_prev = each(lambda i, bi, h: c_ref[i])
    qk = each(lambda i, bi, h: _dot_nt(q[i], k[i]))
    qc = each(lambda i, bi, h: _dot(q[i], c_prev[i].astype(BF16)))

    ig_col = each(lambda i, bi, h: sm_ref[bi, :, h:h + 1])
    b_col = each(lambda i, bi, h: sm_ref[bi, :, M_HEADS + h:M_HEADS + h + 1])
    ig_row = each(lambda i, bi, h: smt_refs[bi][h:h + 1, :])
    b_row = each(lambda i, bi, h: smt_refs[bi][M_HEADS + h:M_HEADS + h + 1, :])
    b_last = each(lambda i, bi, h: b_row[i][:, L - 1:L])
    m_prev = each(lambda i, bi, h: m_ref[i][:, :1])

    dm = each(lambda i, bi, h: jnp.where(causal, b_col[i] - b_row[i] + ig_row[i], -jnp.inf))
    bq = each(lambda i, bi, h: b_col[i] + m_prev[i])
    dm_max = each(lambda i, bi, h: jnp.max(dm[i], axis=1, keepdims=True))
    m_loc = each(lambda i, bi, h: jnp.maximum(bq[i], dm_max[i]))
    inter = each(lambda i, bi, h: jnp.exp(bq[i] - m_loc[i]))
    wt = each(lambda i, bi, h: (jnp.exp(dm[i] - m_loc[i]) * qk[i]).astype(BF16))

    a_row = each(lambda i, bi, h: b_last[i] - b_row[i] + ig_row[i])
    a_max = each(lambda i, bi, h: jnp.max(a_row[i], axis=1, keepdims=True))
    m_new = each(lambda i, bi, h: jnp.maximum(b_last[i] + m_prev[i], a_max[i]))
    decay = each(lambda i, bi, h: jnp.exp(b_last[i] + m_prev[i] - m_new[i]))
    k_t = each(lambda i, bi, h: k[i].astype(F32).T)
    kw_t = each(lambda i, bi, h: (k_t[i] * jnp.exp(a_row[i] - m_new[i])).astype(BF16))

    wv = each(lambda i, bi, h: _dot(wt[i], v[i]))
    kv = each(lambda i, bi, h: _dot(kw_t[i], v[i]))
    for i in range(len(chains)):
        c_ref[i] = decay[i] * c_prev[i] + kv[i]
        m_ref[i] = jnp.broadcast_to(m_new[i], (1, LANES))

    tot = each(lambda i, bi, h: inter[i] * qc[i] + wv[i])
    floor = each(lambda i, bi, h: jnp.exp(-m_loc[i]))
    hh = each(lambda i, bi, h: (tot[i][:, :dh] * scale)
              / jnp.maximum(jnp.abs(tot[i][:, dh:dh + 1]) * scale, floor[i]))
    ms = each(lambda i, bi, h: jnp.mean(hh[i] * hh[i], axis=-1, keepdims=True))
    for i, (bi, h) in enumerate(chains):
        hn = hh[i] * lax.rsqrt(ms[i] + EPS) * gn_ref[:, col(h)]
        out_ref[bi, :, col(h)] = (hn * jax.nn.sigmoid(zo_ref[bi, :, col(h)])).astype(out_ref.dtype)


def _mlstm(zb, zo, sm, smt, gn, batch, seq, L, NBT):
    nc = seq // L
    rows = lambda width: pl.BlockSpec((NBT, L, width), lambda b, c: (b, c, 0))
    smt_specs = [pl.BlockSpec((2 * M_HEADS, L), functools.partial(lambda b, c, j: (0, (b * NBT + j) * nc + c), j=j))
                 for j in range(NBT)]
    nchain = NBT * M_HEADS
    out = pl.pallas_call(
        functools.partial(_mlstm_body, L=L, NBT=NBT),
        grid=(batch // NBT, nc),
        in_specs=[rows(zb.shape[-1]), rows(NO), rows(NSM)] + smt_specs + [_resident((1, M_WIDTH))],
        out_specs=rows(M_WIDTH),
        out_shape=jax.ShapeDtypeStruct((batch, seq, M_WIDTH), BF16),
        scratch_shapes=[
            pltpu.VMEM((nchain, M_HEAD_DIM, 2 * M_HEAD_DIM), F32),
            pltpu.VMEM((nchain, 1, LANES), F32),
        ],
        compiler_params=pltpu.CompilerParams(
            dimension_semantics=("arbitrary", "arbitrary"), vmem_limit_bytes=VMEM_LIMIT),
    )(zb.reshape(batch, seq, -1), zo.reshape(batch, seq, -1), sm.reshape(batch, seq, -1), *([smt] * NBT), gn)
    return out.reshape(batch * seq, M_WIDTH)


def _compress_body(xk_ref, xv_ref, pe_ref, w1_ref, wa_ref, wb_ref, w2_ref, kc_ref, vc_ref, *, ncp):
    for a, (x_ref, out_ref) in enumerate(((xk_ref, kc_ref), (xv_ref, vc_ref))):
        x = x_ref[...]
        first = _dot(x, wa_ref[a])
        second = _dot(x, wb_ref[a])
        second = pltpu.roll(second, ncp - 1, 0)
        pe_term = _dot(pe_ref[a], w1_ref[a])[0:1]
        hid = jax.nn.gelu(first + second + jnp.concatenate([pe_term] * A_KV_GROUPS, axis=1))
        out_ref[...] = _dot(hid.astype(BF16), w2_ref[a]).astype(out_ref.dtype)


def _compress(xk, xv, pe, w1, wa, wb, w2):
    batch, ncp, width = xk.shape
    x_spec = pl.BlockSpec((None, ncp, width), lambda b: (b, 0, 0))
    out_spec = pl.BlockSpec((None, ncp, KV_WIDTH), lambda b: (b, 0, 0))
    return pl.pallas_call(
        functools.partial(_compress_body, ncp=ncp),
        grid=(batch,),
        in_specs=[x_spec, x_spec] + [_resident(a.shape) for a in (pe, w1, wa, wb, w2)],
        out_specs=[out_spec, out_spec],
        out_shape=[jax.ShapeDtypeStruct((batch, ncp, KV_WIDTH), BF16)] * 2,
        compiler_params=pltpu.CompilerParams(
            dimension_semantics=("arbitrary",), vmem_limit_bytes=VMEM_LIMIT),
    )(xk, xv, pe, w1, wa, wb, w2)


V_ROWS = LANES + 16


def _nsa_body(zq_ref, sm_ref, kc_ref, vc_ref, ks_ref, vs_ref, kw_ref, vw_ref, kconst_ref, cconst_ref, out_ref,
              vst_ref, vwt_ref, vct_ref, sa_ref, sb_ref, *, TQ, TK, S, NCP):
    g = pl.program_id(1)
    t0 = pl.program_id(2) * TQ
    R = A_REP
    M = R * TQ
    D = A_HEAD_DIM
    NS = S // SEL_BLOCK
    n_sel = min(N_SELECT, NS)
    lanes = [slice(r * TQ, (r + 1) * TQ) for r in range(R)]
    dt = lax.broadcasted_iota(jnp.int32, (1, TQ), 1)

    @pl.when((g == 0) & (pl.program_id(2) == 0))
    def _():
        vct_ref[...] = vc_ref[...].astype(F32).T.astype(BF16)
        ones_rows = (lax.broadcasted_iota(jnp.int32, (V_ROWS - LANES, 1), 0) == 0).astype(F32)
        for v_ref, vt_ref in ((vs_ref, vst_ref), (vw_ref, vwt_ref)):
            t = vt_ref.shape[2]
            for j in range(S // t):
                v_t = v_ref[j * t:(j + 1) * t, :].astype(F32).T
                vt_ref[j] = jnp.concatenate([v_t, jnp.broadcast_to(ones_rows, (V_ROWS - LANES, t))],
                                            axis=0).astype(BF16)

    slope = jnp.zeros((1, M), F32)
    head = lax.broadcasted_iota(jnp.int32, (1, M), 1) >> (TQ.bit_length() - 1)
    for r in range(R):
        s_r = jnp.where(g == 0, 2.0 ** (-ALIBI_MAX * (r + 1) / A_HEADS),
                        2.0 ** (-ALIBI_MAX * (R + r + 1) / A_HEADS)).astype(F32)
        slope = jnp.where(head == r, s_r, slope)
    row8 = lax.broadcasted_iota(jnp.int32, (8, M), 0)
    feat = jnp.where(row8 == 0, slope * 256.0, jnp.where(row8 == 1, slope, 0.0))
    q_t = zq_ref[...].astype(F32).T
    q_f = jnp.concatenate([q_t[r * D:(r + 1) * D] for r in range(R)], axis=1)
    zeros = lambda n: jnp.zeros((n, M), F32)
    q_grp = jnp.where(g == 0, jnp.concatenate([q_f, zeros(D)], axis=0),
                      jnp.concatenate([zeros(D), q_f], axis=0))

    CW = max(2 * LANES, TQ)
    chains = [slice(c * CW, (c + 1) * CW) for c in range(M // CW)]

    cat = lambda xs: jnp.concatenate(xs, axis=1)
    hpc = CW // TQ

    def chain_scores(k_tile, q_aug):
        return [_dot(k_tile, q_aug[:, ch]) for ch in chains]

    def chain_softmax(scores, mask_t, states):
        blks = [s_t[:, lanes[r]] for s_t in scores for r in range(hpc)]
        if mask_t is not None:
            blks = [blk + mask_t for blk in blks]
        m_new = [jnp.max(blk, axis=0, keepdims=True) for blk in blks]
        alpha = None
        if states is not None:
            m_old = [st[0][:, lanes[r]] for st in states for r in range(hpc)]
            m_new = [jnp.maximum(a, b) for a, b in zip(m_new, m_old)]
            alpha = [jnp.exp(a - b) for a, b in zip(m_old, m_new)]
        probs = [jnp.exp((blk - m).astype(BF16)) for blk, m in zip(blks, m_new)]
        group = lambda xs: [cat(xs[c * hpc:(c + 1) * hpc]) for c in range(len(chains))]
        return group(probs), group(m_new), (group(alpha) if alpha else None)

    def chain_update(v_tile, probs, m_new, alpha, states):
        pvs = [_dot(v_tile, p) for p in probs]
        return tuple((m, a * st[1] + pv) for m, a, st, pv in zip(m_new, alpha, states, pvs))

    def normalise(accs):
        acc = jnp.concatenate(accs, axis=1)
        return jnp.where(g == 0, acc[:D], acc[D:2 * D]) * (1.0 / acc[LANES:LANES + 1])

    q_plain = jnp.concatenate([q_grp, zeros(LANES - D), feat, zeros(LANES - D - 8)], axis=0).astype(BF16)
    wk = min(WINDOW + TQ, S)
    k0 = pl.multiple_of(jnp.maximum(t0 + TQ - wk, 0), TQ)
    k_band = jnp.concatenate([kw_ref[pl.ds(k0, wk), :], kconst_ref[pl.ds(k0, wk), :]], axis=1)
    v_band = vwt_ref[pl.ds(k0 // TQ, wk // TQ)]
    v_band = jnp.concatenate([v_band[j] for j in range(wk // TQ)], axis=1)
    dist = dt - (k0 - t0 + lax.broadcasted_iota(jnp.int32, (wk, 1), 0))
    mask_w = jnp.where((dist >= 0) & (dist < WINDOW), 0.0, NEG)
    cend = lax.broadcasted_iota(jnp.int32, (NCP, 1), 0) * CMP_STRIDE + (CMP_BLOCK - 1) - t0
    mask_c = jnp.where(cend <= dt, 0.0, NEG)

    s_w = chain_scores(k_band, q_plain)
    s_c = _dot(jnp.concatenate([kc_ref[...], cconst_ref[...]], axis=1), q_plain)
    p_w, _, _ = chain_softmax(s_w, mask_w, None)
    p_sum = jnp.zeros((NCP, TQ), F32)
    p_c = []
    for r in range(R):
        blk = s_c[:, lanes[r]] + mask_c
        mx = jnp.maximum(jnp.max(blk, axis=0, keepdims=True), 0.1 * NEG)
        e = jnp.exp(blk - mx)
        p = e * (1.0 / jnp.maximum(jnp.sum(e, axis=0, keepdims=True), 1e-30))
        p_sum = p_sum + p
        p_c.append(p.astype(BF16))
    o_w = normalise([_dot(v_band, p) for p in p_w])
    o_c = _dot(vct_ref[...], cat(p_c))
    o_c = jnp.where(g == 0, o_c[:D], o_c[D:])

    jc = lax.broadcasted_iota(jnp.int32, (NS, 1), 0)
    nr = lax.broadcasted_iota(jnp.int32, (1, NCP), 1)
    overlap_t = ((nr * CMP_STRIDE <= jc * SEL_BLOCK + (SEL_BLOCK - 1))
                 & (nr * CMP_STRIDE + (CMP_BLOCK - 1) >= jc * SEL_BLOCK))
    imp_t = _dot(jnp.where(overlap_t, 1.0, 0.0).astype(BF16), p_sum.astype(BF16))
    cur = (t0 + dt) >> (SEL_BLOCK.bit_length() - 1)
    forced = (jc == 0) | (jc == cur) | (jc == cur - 1)
    score = jnp.where(jc <= cur, jnp.where(forced, BIG, imp_t), NEG)
    sub = lax.broadcasted_iota(jnp.int32, (8, TQ), 0)
    blocks = [score[8 * jb:8 * jb + 8] for jb in range(NS // 8)]
    ranks = [jnp.zeros((8, TQ), F32) for _ in blocks]
    for i in range(NS):
        s_i = score[i:i + 1, :]
        for jb, blk in enumerate(blocks):
            if 8 * jb > i:
                beats = jnp.where(s_i >= blk, 1.0, 0.0)
            elif 8 * jb + 7 < i:
                beats = jnp.where(s_i > blk, 1.0, 0.0)
            else:
                beats = jnp.where(sub + 8 * jb > i, jnp.where(s_i >= blk, 1.0, 0.0), jnp.where(s_i > blk, 1.0, 0.0))
            ranks[jb] = ranks[jb] + beats
    sel_t = jnp.where(jnp.concatenate(ranks, axis=0) < n_sel, 0.0, NEG)
    sel_rows = [jnp.concatenate([sel_t] * R, axis=1)]
    if NS < LANES - D:
        sel_rows.append(zeros(LANES - D - NS))
    q_sel = jnp.concatenate([q_grp] + sel_rows + [feat, zeros(LANES - D - 8)], axis=0).astype(BF16)

    def issue_scores(kt, dst_ref):
        rows = pl.ds(pl.multiple_of(kt * TK, TK), TK)
        k_tile = jnp.concatenate([ks_ref[rows, :], kconst_ref[rows, :]], axis=1)
        for c, s_t in enumerate(chain_scores(k_tile, q_sel)):
            dst_ref[c] = s_t

    def consume(src_ref, kt, mask_t, states):
        probs, m_new, alpha = chain_softmax([src_ref[c] for c in range(len(chains))], mask_t, states)
        return chain_update(vst_ref[kt], probs, m_new, alpha, states)

    def pair(j, states):
        issue_scores(2 * j + 1, sb_ref)
        states = consume(sa_ref, 2 * j, None, states)
        issue_scores(2 * j + 2, sa_ref)
        return consume(sb_ref, 2 * j + 1, None, states)

    kd = t0 // TK
    mask_d = jnp.where(kd * TK - t0 + lax.broadcasted_iota(jnp.int32, (TK, 1), 0) <= dt, 0.0, NEG)
    init = tuple((jnp.full((1, CW), NEG, F32), jnp.zeros((V_ROWS, CW), F32)) for _ in chains)
    issue_scores(0, sa_ref)
    states = lax.fori_loop(0, kd // 2, pair, init)

    def odd_tail(states):
        issue_scores(kd, sb_ref)
        states = consume(sa_ref, kd - 1, None, states)
        return consume(sb_ref, kd, mask_d, states)

    states = lax.cond(kd % 2 == 1, odd_tail, lambda st: consume(sa_ref, kd, mask_d, st), states)
    o_s = normalise([acc for _, acc in states])

    sm_t = sm_ref[...].T
    mixed = []
    for r in range(R):
        gates = []
        for c in range(3):
            c0 = AG_OFF + r * 3 + c
            c1 = AG_OFF + (R + r) * 3 + c
            gates.append(jax.nn.sigmoid(jnp.where(g == 0, sm_t[c0:c0 + 1, :], sm_t[c1:c1 + 1, :])))
        mixed.append(gates[0] * o_c[:, lanes[r]] + gates[1] * o_s[:, lanes[r]] + gates[2] * o_w[:, lanes[r]])
    for r in range(0, R, 2):
        pair = jnp.concatenate([mixed[r], mixed[r + 1]], axis=0).T
        out_ref[:, r * D:(r + 2) * D] = pair.astype(out_ref.dtype)


def _nsa(zaq, sm, kc, vc, zkv, kconst, cconst, batch, seq, TQ, TK):
    n = batch * seq
    G = A_KV_GROUPS
    nq = seq // TQ
    ncp = kc.shape[1]
    per_batch = lambda a: pl.BlockSpec((None,) + a.shape[1:], lambda b, g, i: (b, 0, 0))
    kv_block = lambda j: pl.BlockSpec((seq, KV_WIDTH), functools.partial(lambda b, g, i, j: (b, j), j=j))
    return pl.pallas_call(
        functools.partial(_nsa_body, TQ=TQ, TK=TK, S=seq, NCP=ncp),
        grid=(batch, G, nq),
        in_specs=[
            pl.BlockSpec((TQ, A_REP * A_HEAD_DIM), lambda b, g, i: (b * nq + i, g)),
            pl.BlockSpec((TQ, NSM), lambda b, g, i: (b * nq + i, 0)),
            per_batch(kc), per_batch(vc),
            kv_block(0), kv_block(1), kv_block(2), kv_block(3),
            _resident(kconst.shape), _resident(cconst.shape),
        ],
        out_specs=pl.BlockSpec((TQ, A_REP * A_HEAD_DIM), lambda b, g, i: (b * nq + i, g)),
        out_shape=jax.ShapeDtypeStruct((n, A_WIDTH), BF16),
        scratch_shapes=[pltpu.VMEM((seq // TK, V_ROWS, TK), BF16), pltpu.VMEM((seq // TQ, V_ROWS, TQ), BF16),
                        pltpu.VMEM((KV_WIDTH, ncp), BF16)]
        + [pltpu.VMEM((A_REP * TQ // max(2 * LANES, TQ), TK, max(2 * LANES, TQ)), F32)] * 2,
        compiler_params=pltpu.CompilerParams(
            dimension_semantics=("arbitrary", "arbitrary", "arbitrary"), vmem_limit_bytes=VMEM_LIMIT),
    )(zaq, sm, kc, vc, zkv, zkv, zkv, zkv, kconst, cconst)


def _outblock_body(x_ref, mo_ref, ao_ref, p_ref, wout_ref, ln2_ref, wup_ref, cw_ref, cb_ref, wdn_ref,
                   pg_ref, wpg_ref, wpp_ref, fg_ref, out_ref, tail_ref, *, tm, tiles_per_seq, fc):
    @pl.when(pl.program_id(0) % tiles_per_seq == 0)
    def _():
        tail_ref[...] = jnp.zeros_like(tail_ref)

    x1 = (x_ref[...] + _dot(mo_ref[...], wout_ref[:M_WIDTH, :]) + _dot(ao_ref[...], wout_ref[M_WIDTH:, :]))
    ple = _dot(p_ref[...].astype(BF16), wpp_ref[...])
    h2 = _rms(x1, ln2_ref[...]).astype(BF16)
    rowi = lax.broadcasted_iota(jnp.int32, (tm, 1), 0)
    acc = jnp.zeros(x1.shape, F32)
    up = lambda c0: (_dot(h2, wup_ref[:, c0:c0 + fc]), _dot(h2, wup_ref[:, D_FF + c0:D_FF + c0 + fc]))
    nxt = up(0)
    for c0 in range(0, D_FF, fc):
        gate, val = nxt
        if c0 + fc < D_FF:
            nxt = up(c0 + fc)
        tail = tail_ref[:, c0:c0 + fc]
        g_m1 = jnp.where(rowi == 0, tail[7:8], pltpu.roll(gate, 1, 0))
        g_m2 = jnp.where(rowi == 0, tail[6:7], jnp.where(rowi == 1, tail[7:8], pltpu.roll(gate, 2, 0)))
        conv = (cw_ref[0:1, c0:c0 + fc] * g_m2 + cw_ref[1:2, c0:c0 + fc] * g_m1
                + cw_ref[2:3, c0:c0 + fc] * gate + cb_ref[:, c0:c0 + fc])
        y = jax.nn.silu(conv) * val
        acc = acc + _dot(y.astype(BF16), wdn_ref[c0:c0 + fc, :])
        tail_ref[:, c0:c0 + fc] = gate[tm - 8:tm]
    x2 = x1 + acc
    gate2 = jax.nn.sigmoid(_dot(_rms(x2, pg_ref[...]).astype(BF16), wpg_ref[...]))
    x3 = x2 + gate2 * ple
    out_ref[...] = _rms(x3, fg_ref[...])


def _outblock(x2d, mo, ao, p2d, wout, ln2, wup, cw, cb, wdn, pg, wpg, wpp, fg, seq, tm):
    n, d = x2d.shape
    row = lambda width: pl.BlockSpec((tm, width), lambda i: (i, 0))
    return pl.pallas_call(
        functools.partial(_outblock_body, tm=tm, tiles_per_seq=seq // tm, fc=512),
        grid=(n // tm,),
        in_specs=[
            row(d), row(M_WIDTH), row(A_WIDTH), row(p2d.shape[1]),
            _resident(wout.shape), _resident((1, d)), _resident(wup.shape), _resident(cw.shape),
            _resident((1, D_FF)), _resident(wdn.shape), _resident((1, d)), _resident(wpg.shape),
            _resident(wpp.shape), _resident((1, d)),
        ],
        out_specs=row(d),
        out_shape=jax.ShapeDtypeStruct((n, d), F32),
        scratch_shapes=[pltpu.VMEM((8, D_FF), F32)],
        compiler_params=pltpu.CompilerParams(
            dimension_semantics=("arbitrary",), vmem_limit_bytes=VMEM_LIMIT),
    )(x2d, mo, ao, p2d, wout, ln2, wup, cw, cb, wdn, pg, wpg, wpp, fg)


def _layer(x2d, p2d, batch, seq, ln1_g, w_in, gate_bias, mnorm_g, pe_k, pe_v, ck_w1, ck_w2, cv_w1, cv_w2,
           w_out, ln2_g, w_up, conv_w, conv_b, w_down, ple_g, w_pg, w_pp, out_g):
    d = x2d.shape[1]
    G = A_KV_GROUPS
    dh = A_HEAD_DIM
    o_mo = 3 * M_WIDTH
    o_mi = o_mo + M_WIDTH
    o_aq = o_mi + 2 * M_HEADS
    o_ag = o_aq + A_WIDTH + 6 * KV_WIDTH
    w = jnp.concatenate([
        w_in[:, :o_mo], w_in[:, o_aq:o_aq + A_WIDTH] * (dh ** -0.5), w_in[:, o_aq + A_WIDTH:o_ag],
        w_in[:, o_mo:o_mi],
        w_in[:, o_mi:o_aq], w_in[:, o_ag:], jnp.zeros((d, NSM - 2 * M_HEADS - 3 * A_HEADS), w_in.dtype),
    ], axis=1).astype(BF16)
    bias = jnp.concatenate([gate_bias, jnp.zeros((NSM - 2 * M_HEADS,), F32)])[None, :]

    L = min(128, seq)
    zmq, zaq, zkc, zvc, zkv, zo, sm, smt = _inproj(x2d, ln1_g[None, :], w, bias, tm=512, chunk=L)
    m_out = _mlstm(zmq, zo, sm, smt, mnorm_g[None, :], batch, seq, L=L, NBT=2 if batch % 2 == 0 else 1)

    TQ, TK = 256, min(512, seq)
    ncp = seq // CMP_STRIDE
    w1 = jnp.stack([ck_w1, cv_w1])
    eye = jnp.eye(G, dtype=F32)
    w1_blk = (w1.reshape(2, 2, CMP_STRIDE, 1, dh, 1, CMP_HIDDEN) * eye[None, None, None, :, None, :, None]
              ).reshape(2, 2, CMP_STRIDE * G * dh, G * CMP_HIDDEN).astype(BF16)
    w2 = jnp.stack([ck_w2, cv_w2])
    w2_blk = (w2[:, None, :, None, :] * eye[None, :, None, :, None]).reshape(
        2, G * CMP_HIDDEN, G * dh).astype(BF16)
    pe = jnp.stack([pe_k, pe_v]).reshape(2, 1, CMP_BLOCK * dh)
    pe = jnp.broadcast_to(pe, (2, 8, CMP_BLOCK * dh)).astype(BF16)
    row_view = lambda z: z.reshape(batch, ncp, CMP_STRIDE * KV_WIDTH)
    kc, vc = _compress(row_view(zkc), row_view(zvc), pe, w1.astype(BF16), w1_blk[:, 0], w1_blk[:, 1], w2_blk)

    pos = jnp.arange(seq)[:, None]
    onehot = pos // SEL_BLOCK == jnp.arange(LANES - dh)[None, :]
    pad = lambda n: jnp.zeros((n, LANES - dh - 2), jnp.int32)
    kconst = jnp.concatenate([onehot, pos // 256, pos % 256, pad(seq)], axis=1).astype(BF16)
    cpos = jnp.arange(ncp)[:, None] * CMP_STRIDE
    cconst = jnp.concatenate([jnp.zeros((ncp, LANES - dh + 1), jnp.int32), cpos, pad(ncp)], axis=1).astype(BF16)
    a_out = _nsa(zaq, sm, kc, vc, zkv, kconst, cconst, batch, seq, TQ=TQ, TK=TK)

    cw = jnp.concatenate([conv_w, jnp.zeros((8 - conv_w.shape[0], D_FF), F32)], axis=0)
    return _outblock(x2d, m_out, a_out, p2d, w_out.astype(BF16), ln2_g[None, :], w_up.astype(BF16), cw,
                     conv_b[None, :], w_down.astype(BF16), ple_g[None, :], w_pg.astype(BF16),
                     w_pp.astype(BF16), out_g[None, :], seq, tm=512)


def kernel(x, p, ln1_g, w_in, mlstm_gate_bias, mlstm_norm_g, cmp_pos_k, cmp_pos_v, cmp_k_w1, cmp_k_w2,
           cmp_v_w1, cmp_v_w2, w_out, ln2_g, w_up, conv_w, conv_b, w_down, ple_norm_g, w_ple_gate,
           w_ple_proj, final_g):
    batch, seq, d = x.shape
    depth = w_in.shape[0]
    assert depth == 1, "the fused output block applies the final norm, so a single layer is supported"
    assert seq % 128 == 0 and WINDOW + 128 <= seq <= SEL_BLOCK * (LANES - A_HEAD_DIM)
    x2d = x.reshape(batch * seq, d)
    i = 0
    out = _layer(x2d, p[i].reshape(batch * seq, -1), batch, seq, ln1_g[i], w_in[i], mlstm_gate_bias[i],
                 mlstm_norm_g[i], cmp_pos_k[i], cmp_pos_v[i], cmp_k_w1[i], cmp_k_w2[i], cmp_v_w1[i],
                 cmp_v_w2[i], w_out[i], ln2_g[i], w_up[i], conv_w[i], conv_b[i], w_down[i], ple_norm_g[i],
                 w_ple_gate[i], w_ple_proj[i], final_g)
    return out.reshape(batch, seq, d)
```

```python
import functools

import jax
import jax.numpy as jnp
from jax import lax
from jax.experimental import pallas as pl
from jax.experimental.pallas import tpu as pltpu

F32 = jnp.float32
BF16 = jnp.bfloat16

M_HEADS = 4
M_HEAD_DIM = 128
M_WIDTH = M_HEADS * M_HEAD_DIM
A_HEADS = 8
A_KV_GROUPS = 2
A_REP = A_HEADS // A_KV_GROUPS
A_HEAD_DIM = 64
A_WIDTH = A_HEADS * A_HEAD_DIM
KV_WIDTH = A_KV_GROUPS * A_HEAD_DIM
CMP_BLOCK = 32
CMP_STRIDE = 16
CMP_HIDDEN = 128
SEL_BLOCK = 64
N_SELECT = 16
WINDOW = 512
ALIBI_MAX = 8.0
D_FF = 2048
EPS = 1e-6
NEG = -1e30
BIG = 1e30
LOG2E = 1.4426950408889634

LANES = 128
VMEM_LIMIT = 56 * 1024 * 1024

NB = 3 * M_WIDTH + A_WIDTH + 6 * KV_WIDTH
NO = M_WIDTH
NSM = LANES
AQ_OFF = 3 * M_WIDTH
KV_OFF = AQ_OFF + A_WIDTH
AG_OFF = 2 * M_HEADS


def _dot(a, b):
    return jnp.dot(a, b, preferred_element_type=F32)


def _dot_nt(a, b):
    return lax.dot_general(a, b, (((1,), (1,)), ((), ())), preferred_element_type=F32)


def _rms(x, g):
    return x * lax.rsqrt(jnp.mean(x * x, axis=-1, keepdims=True) + EPS) * g


def _log_sigmoid(x):
    return jnp.minimum(x, 0.0) - jnp.log1p(jnp.exp(-jnp.abs(x)))


def _resident(shape):
    zeros = (0,) * len(shape)
    return pl.BlockSpec(shape, lambda *_: zeros, pipeline_mode=pl.Buffered(1))


def _inproj_body(x_ref, g_ref, w_ref, bias_ref, tri_ref, zm_ref, zq_ref, zkc_ref, zvc_ref, zkv_ref, zo_ref,
                 sm_ref, smt_ref):
    h = _rms(x_ref[...], g_ref[...]).astype(BF16)
    sm_t = (_dot(h, w_ref[:, NB + NO:]) + bias_ref[...]).T
    rows = 4 * M_HEADS
    gate = sm_t[:rows]
    row = lax.broadcasted_iota(jnp.int32, gate.shape, 0)
    is_f = (row >= M_HEADS) & (row < 2 * M_HEADS)
    rem = _log_sigmoid(gate)
    parts = []
    for _ in range(3):
        parts.append(rem.astype(BF16))
        rem = rem - parts[-1].astype(F32)

    cw = 512
    for ref, base in ((zm_ref, 0), (zq_ref, AQ_OFF), (zkc_ref, KV_OFF), (zvc_ref, KV_OFF + KV_WIDTH),
                      (zkv_ref, KV_OFF + 2 * KV_WIDTH)):
        for c0 in range(0, ref.shape[1], cw):
            c1 = min(c0 + cw, ref.shape[1])
            ref[:, c0:c1] = _dot(h, w_ref[:, base + c0:base + c1]).astype(BF16)
    zo_ref[...] = _dot(h, w_ref[:, NB:NB + NO])

    cum = _dot(parts[0], tri_ref[...]) + _dot(parts[1], tri_ref[...]) + _dot(parts[2], tri_ref[...])
    gate = jnp.where(is_f, cum, gate)
    smt_ref[...] = gate[:2 * M_HEADS]
    sm_ref[...] = jnp.concatenate([gate, sm_t[rows:]], axis=0).T


def _inproj(x2d, g, w, bias, tm, chunk):
    n, d = x2d.shape
    idx = jnp.arange(tm)
    tri = ((idx[:, None] // chunk == idx[None, :] // chunk) & (idx[:, None] <= idx[None, :])).astype(BF16)
    return pl.pallas_call(
        _inproj_body,
        grid=(n // tm,),
        in_specs=[
            pl.BlockSpec((tm, d), lambda i: (i, 0)),
            _resident((1, d)),
            _resident(w.shape),
            _resident((1, NSM)),
            _resident((tm, tm)),
        ],
        out_specs=[
            pl.BlockSpec((tm, AQ_OFF), lambda i: (i, 0)),
            pl.BlockSpec((tm, KV_OFF - AQ_OFF), lambda i: (i, 0)),
            pl.BlockSpec((tm, KV_WIDTH), lambda i: (i, 0)),
            pl.BlockSpec((tm, KV_WIDTH), lambda i: (i, 0)),
            pl.BlockSpec((tm, NB - KV_OFF - 2 * KV_WIDTH), lambda i: (i, 0)),
            pl.BlockSpec((tm, NO), lambda i: (i, 0)),
            pl.BlockSpec((tm, NSM), lambda i: (i, 0)),
            pl.BlockSpec((2 * M_HEADS, tm), lambda i: (0, i)),
        ],
        out_shape=[
            jax.ShapeDtypeStruct((n, AQ_OFF), BF16),
            jax.ShapeDtypeStruct((n, KV_OFF - AQ_OFF), BF16),
            jax.ShapeDtypeStruct((n, KV_WIDTH), BF16),
            jax.ShapeDtypeStruct((n, KV_WIDTH), BF16),
            jax.ShapeDtypeStruct((n, NB - KV_OFF - 2 * KV_WIDTH), BF16),
            jax.ShapeDtypeStruct((n, NO), F32),
            jax.ShapeDtypeStruct((n, NSM), F32),
            jax.ShapeDtypeStruct((2 * M_HEADS, n), F32),
        ],
        compiler_params=pltpu.CompilerParams(
            dimension_semantics=("arbitrary",), vmem_limit_bytes=VMEM_LIMIT),
    )(x2d, g, w, bias, tri)


def _mlstm_body(zb_ref, zo_ref, sm_ref, *rest, L, NBT):
    smt_refs, (gn_ref, out_ref, c_ref, m_ref) = rest[:NBT], rest[NBT:]

    @pl.when(pl.program_id(1) == 0)
    def _():
        c_ref[...] = jnp.zeros_like(c_ref)
        m_ref[...] = jnp.zeros_like(m_ref)

    dh = M_HEAD_DIM
    scale = dh ** -0.5
    causal = lax.broadcasted_iota(jnp.int32, (L, L), 1) <= lax.broadcasted_iota(jnp.int32, (L, L), 0)
    chains = [(bi, h) for bi in range(NBT) for h in range(M_HEADS)]
    each = lambda f: [f(i, bi, h) for i, (bi, h) in enumerate(chains)]
    col = lambda h: slice(h * dh, (h + 1) * dh)
    ones = (lax.broadcasted_iota(jnp.int32, (L, LANES), 1) == 0).astype(BF16)

    q = each(lambda i, bi, h: zb_ref[bi, :, col(h)])
    k = each(lambda i, bi, h: zb_ref[bi, :, M_WIDTH + h * dh:M_WIDTH + (h + 1) * dh])
    v = each(lambda i, bi, h: jnp.concatenate([zb_ref[bi, :, 2 * M_WIDTH + h * dh:2 * M_WIDTH + (h + 1) * dh],
                                               ones], axis=1))
    c_prev = each(lambda i, bi, h: c_ref[i])
    qk = each(lambda i, bi, h: _dot_nt(q[i], k[i]))
    qc = each(lambda i, bi, h: _dot(q[i], c_prev[i].astype(BF16)))

    ig_col = each(lambda i, bi, h: sm_ref[bi, :, h:h + 1])
    b_col = each(lambda i, bi, h: sm_ref[bi, :, M_HEADS + h:M_HEADS + h + 1])
    ig_row = each(lambda i, bi, h: smt_refs[bi][h:h + 1, :])
    b_row = each(lambda i, bi, h: smt_refs[bi][M_HEADS + h:M_HEADS + h + 1, :])
    b_last = each(lambda i, bi, h: b_row[i][:, L - 1:L])
    m_prev = each(lambda i, bi, h: m_ref[i][:, :1])

    dm = each(lambda i, bi, h: jnp.where(causal, b_col[i] - b_row[i] + ig_row[i], -jnp.inf))
    bq = each(lambda i, bi, h: b_col[i] + m_prev[i])
    dm_max = each(lambda i, bi, h: jnp.max(dm[i], axis=1, keepdims=True))
    m_loc = each(lambda i, bi, h: jnp.maximum(bq[i], dm_max[i]))
    inter = each(lambda i, bi, h: jnp.exp(bq[i] - m_loc[i]))
    wt = each(lambda i, bi, h: (jnp.exp(dm[i] - m_loc[i]) * qk[i]).astype(BF16))

    a_row = each(lambda i, bi, h: b_last[i] - b_row[i] + ig_row[i])
    a_max = each(lambda i, bi, h: jnp.max(a_row[i], axis=1, keepdims=True))
    m_new = each(lambda i, bi, h: jnp.maximum(b_last[i] + m_prev[i], a_max[i]))
    decay = each(lambda i, bi, h: jnp.exp(b_last[i] + m_prev[i] - m_new[i]))
    k_t = each(lambda i, bi, h: k[i].astype(F32).T)
    kw_t = each(lambda i, bi, h: (k_t[i] * jnp.exp(a_row[i] - m_new[i])).astype(BF16))

    wv = each(lambda i, bi, h: _dot(wt[i], v[i]))
    kv = each(lambda i, bi, h: _dot(kw_t[i], v[i]))
    for i in range(len(chains)):
        c_ref[i] = decay[i] * c_prev[i] + kv[i]
        m_ref[i] = jnp.broadcast_to(m_new[i], (1, LANES))

    tot = each(lambda i, bi, h: inter[i] * qc[i] + wv[i])
    floor = each(lambda i, bi, h: jnp.exp(-m_loc[i]))
    hh = each(lambda i, bi, h: (tot[i][:, :dh] * scale)
              / jnp.maximum(jnp.abs(tot[i][:, dh:dh + 1]) * scale, floor[i]))
    ms = each(lambda i, bi, h: jnp.mean(hh[i] * hh[i], axis=-1, keepdims=True))
    for i, (bi, h) in enumerate(chains):
        hn = hh[i] * lax.rsqrt(ms[i] + EPS) * gn_ref[:, col(h)]
        out_ref[bi, :, col(h)] = (hn * jax.nn.sigmoid(zo_ref[bi, :, col(h)])).astype(out_ref.dtype)


def _mlstm(zb, zo, sm, smt, gn, batch, seq, L, NBT):
    nc = seq // L
    rows = lambda width: pl.BlockSpec((NBT, L, width), lambda b, c: (b, c, 0))
    smt_specs = [pl.BlockSpec((2 * M_HEADS, L), functools.partial(lambda b, c, j: (0, (b * NBT + j) * nc + c), j=j))
                 for j in range(NBT)]
    nchain = NBT * M_HEADS
    out = pl.pallas_call(
        functools.partial(_mlstm_body, L=L, NBT=NBT),
        grid=(batch // NBT, nc),
        in_specs=[rows(zb.shape[-1]), rows(NO), rows(NSM)] + smt_specs + [_resident((1, M_WIDTH))],
        out_specs=rows(M_WIDTH),
        out_shape=jax.ShapeDtypeStruct((batch, seq, M_WIDTH), BF16),
        scratch_shapes=[
            pltpu.VMEM((nchain, M_HEAD_DIM, 2 * M_HEAD_DIM), F32),
            pltpu.VMEM((nchain, 1, LANES), F32),
        ],
        compiler_params=pltpu.CompilerParams(
            dimension_semantics=("arbitrary", "arbitrary"), vmem_limit_bytes=VMEM_LIMIT),
    )(zb.reshape(batch, seq, -1), zo.reshape(batch, seq, -1), sm.reshape(batch, seq, -1), *([smt] * NBT), gn)
    return out.reshape(batch * seq, M_WIDTH)


def _compress_body(xk_ref, xv_ref, pe_ref, w1_ref, wa_ref, wb_ref, w2_ref, kc_ref, vc_ref, *, ncp):
    for a, (x_ref, out_ref) in enumerate(((xk_ref, kc_ref), (xv_ref, vc_ref))):
        x = x_ref[...]
        first = _dot(x, wa_ref[a])
        second = _dot(x, wb_ref[a])
        second = pltpu.roll(second, ncp - 1, 0)
        pe_term = _dot(pe_ref[a], w1_ref[a])[0:1]
        hid = jax.nn.gelu(first + second + jnp.concatenate([pe_term] * A_KV_GROUPS, axis=1))
        out_ref[...] = _dot(hid.astype(BF16), w2_ref[a]).astype(out_ref.dtype)


def _compress(xk, xv, pe, w1, wa, wb, w2):
    batch, ncp, width = xk.shape
    x_spec = pl.BlockSpec((None, ncp, width), lambda b: (b, 0, 0))
    out_spec = pl.BlockSpec((None, ncp, KV_WIDTH), lambda b: (b, 0, 0))
    return pl.pallas_call(
        functools.partial(_compress_body, ncp=ncp),
        grid=(batch,),
        in_specs=[x_spec, x_spec] + [_resident(a.shape) for a in (pe, w1, wa, wb, w2)],
        out_specs=[out_spec, out_spec],
        out_shape=[jax.ShapeDtypeStruct((batch, ncp, KV_WIDTH), BF16)] * 2,
        compiler_params=pltpu.CompilerParams(
            dimension_semantics=("arbitrary",), vmem_limit_bytes=VMEM_LIMIT),
    )(xk, xv, pe, w1, wa, wb, w2)


V_ROWS = A_HEAD_DIM + 16


def _nsa_body(zq_ref, sm_ref, kc_ref, vc_ref, ks_ref, vs_ref, kw_ref, vw_ref, kconst_ref, cconst_ref, out_ref,
              vst_ref, vwt_ref, vct_ref, sa_ref, sb_ref, *, TQ, TK, S, NCP):
    g = pl.program_id(1)
    t0 = pl.program_id(2) * TQ
    R = A_REP
    M = R * TQ
    D = A_HEAD_DIM
    NS = S // SEL_BLOCK
    n_sel = min(N_SELECT, NS)
    lanes = [slice(r * TQ, (r + 1) * TQ) for r in range(R)]
    dt = lax.broadcasted_iota(jnp.int32, (1, TQ), 1)

    @pl.when((g == 0) & (pl.program_id(2) == 0))
    def _():
        vc_t = vc_ref[...].astype(F32).T.astype(BF16)
        ones_rows = (lax.broadcasted_iota(jnp.int32, (V_ROWS - D, 1), 0) == 0).astype(F32)
        for gi in range(A_KV_GROUPS):
            vct_ref[gi] = vc_t[gi * D:(gi + 1) * D]
        for v_ref, vt_ref in ((vs_ref, vst_ref), (vw_ref, vwt_ref)):
            t = vt_ref.shape[3]
            for j in range(S // t):
                v_t = v_ref[j * t:(j + 1) * t, :].astype(F32).T
                for gi in range(A_KV_GROUPS):
                    vt_ref[j, gi] = jnp.concatenate([v_t[gi * D:(gi + 1) * D],
                                                     jnp.broadcast_to(ones_rows, (V_ROWS - D, t))],
                                                    axis=0).astype(BF16)

    slope = jnp.zeros((1, M), F32)
    head = lax.broadcasted_iota(jnp.int32, (1, M), 1) >> (TQ.bit_length() - 1)
    for r in range(R):
        s_r = jnp.where(g == 0, 2.0 ** (-ALIBI_MAX * (r + 1) / A_HEADS),
                        2.0 ** (-ALIBI_MAX * (R + r + 1) / A_HEADS)).astype(F32)
        slope = jnp.where(head == r, s_r, slope)
    row8 = lax.broadcasted_iota(jnp.int32, (8, M), 0)
    rem = slope * LOG2E
    feat = jnp.zeros((8, M), F32)
    for j in range(3):
        term = rem.astype(BF16).astype(F32)
        feat = jnp.where(row8 == 2 * j, term * 256.0, jnp.where(row8 == 2 * j + 1, term, feat))
        rem = rem - term
    q_t = zq_ref[...].astype(F32).T
    q_f = jnp.concatenate([q_t[r * D:(r + 1) * D] for r in range(R)], axis=1)
    zeros = lambda n: jnp.zeros((n, M), F32)
    q_grp = jnp.where(g == 0, jnp.concatenate([q_f, zeros(D)], axis=0),
                      jnp.concatenate([zeros(D), q_f], axis=0))

    CW = max(2 * LANES, TQ)
    chains = [slice(c * CW, (c + 1) * CW) for c in range(M // CW)]

    cat = lambda xs: jnp.concatenate(xs, axis=1)
    hpc = CW // TQ

    def chain_scores(k_tile, q_aug):
        return [_dot(k_tile, q_aug[:, ch]) for ch in chains]

    def chain_softmax(scores, mask_t, states):
        blks = [s_t[:, lanes[r]] for s_t in scores for r in range(hpc)]
        if mask_t is not None:
            blks = [blk + mask_t for blk in blks]
        m_new = [jnp.max(blk, axis=0, keepdims=True) for blk in blks]
        alpha = None
        if states is not None:
            m_old = [st[0][:, lanes[r]] for st in states for r in range(hpc)]
            m_new = [jnp.maximum(a, b) for a, b in zip(m_new, m_old)]
            alpha = [jnp.exp2(a - b) for a, b in zip(m_old, m_new)]
        probs = [jnp.exp2((blk - m).astype(BF16)) for blk, m in zip(blks, m_new)]
        group = lambda xs: [cat(xs[c * hpc:(c + 1) * hpc]) for c in range(len(chains))]
        return group(probs), group(m_new), (group(alpha) if alpha else None)

    def chain_update(v_tile, probs, m_new, alpha, states):
        pvs = [_dot(v_tile, p) for p in probs]
        return tuple((m, a * st[1] + pv) for m, a, st, pv in zip(m_new, alpha, states, pvs))

    def normalise(accs):
        acc = jnp.concatenate(accs, axis=1)
        return acc[:D] * (1.0 / acc[D:D + 1])

    q_plain = jnp.concatenate([q_grp, zeros(LANES - D), feat, zeros(LANES - D - 8)], axis=0).astype(BF16)
    wk = min(WINDOW + TQ, S)
    k0 = pl.multiple_of(jnp.maximum(t0 + TQ - wk, 0), TQ)
    k_band = jnp.concatenate([kw_ref[pl.ds(k0, wk), :], kconst_ref[pl.ds(k0, wk), :]], axis=1)
    v_band = vwt_ref[pl.ds(k0 // TQ, wk // TQ), g]
    v_band = jnp.concatenate([v_band[j] for j in range(wk // TQ)], axis=1)
    dist = dt - (k0 - t0 + lax.broadcasted_iota(jnp.int32, (wk, 1), 0))
    mask_w = jnp.where((dist >= 0) & (dist < WINDOW), 0.0, NEG)
    cend = lax.broadcasted_iota(jnp.int32, (NCP, 1), 0) * CMP_STRIDE + (CMP_BLOCK - 1) - t0
    mask_c = jnp.where(cend <= dt, 0.0, NEG)

    s_w = chain_scores(k_band, q_plain)
    s_c = _dot(jnp.concatenate([kc_ref[...], cconst_ref[...]], axis=1), q_plain)
    p_w, _, _ = chain_softmax(s_w, mask_w, None)
    p_sum = jnp.zeros((NCP, TQ), F32)
    p_c = []
    for r in range(R):
        blk = s_c[:, lanes[r]] + mask_c
        mx = jnp.maximum(jnp.max(blk, axis=0, keepdims=True), 0.1 * NEG)
        e = jnp.exp2(blk - mx)
        p = e * (1.0 / jnp.maximum(jnp.sum(e, axis=0, keepdims=True), 1e-30))
        p_sum = p_sum + p
        p_c.append(p.astype(BF16))
    o_w = normalise([_dot(v_band, p) for p in p_w])
    o_c = _dot(vct_ref[g], cat(p_c))

    jc = lax.broadcasted_iota(jnp.int32, (NS, 1), 0)
    nr = lax.broadcasted_iota(jnp.int32, (1, NCP), 1)
    overlap_t = ((nr * CMP_STRIDE <= jc * SEL_BLOCK + (SEL_BLOCK - 1))
                 & (nr * CMP_STRIDE + (CMP_BLOCK - 1) >= jc * SEL_BLOCK))
    imp_t = _dot(jnp.where(overlap_t, 1.0, 0.0).astype(BF16), p_sum.astype(BF16))
    cur = (t0 + dt) >> (SEL_BLOCK.bit_length() - 1)
    forced = (jc == 0) | (jc == cur) | (jc == cur - 1)
    score = jnp.where(jc <= cur, jnp.where(forced, BIG, imp_t), NEG)
    sub = lax.broadcasted_iota(jnp.int32, (8, TQ), 0)
    blocks = [score[8 * jb:8 * jb + 8] for jb in range(NS // 8)]
    ranks = [jnp.zeros((8, TQ), F32) for _ in blocks]
    for i in range(NS):
        s_i = score[i:i + 1, :]
        for jb, blk in enumerate(blocks):
            if 8 * jb > i:
                beats = jnp.where(s_i >= blk, 1.0, 0.0)
            elif 8 * jb + 7 < i:
                beats = jnp.where(s_i > blk, 1.0, 0.0)
            else:
                beats = jnp.where(sub + 8 * jb > i, jnp.where(s_i >= blk, 1.0, 0.0), jnp.where(s_i > blk, 1.0, 0.0))
            ranks[jb] = ranks[jb] + beats
    sel_t = jnp.where(jnp.concatenate(ranks, axis=0) < n_sel, 0.0, NEG)
    sel_rows = [jnp.concatenate([sel_t] * R, axis=1)]
    if NS < LANES - D:
        sel_rows.append(zeros(LANES - D - NS))
    q_sel = jnp.concatenate([q_grp] + sel_rows + [feat, zeros(LANES - D - 8)], axis=0).astype(BF16)

    def issue_scores(kt, dst_ref):
        rows = pl.ds(pl.multiple_of(kt * TK, TK), TK)
        k_tile = jnp.concatenate([ks_ref[rows, :], kconst_ref[rows, :]], axis=1)
        for c, s_t in enumerate(chain_scores(k_tile, q_sel)):
            dst_ref[c] = s_t

    def consume(src_ref, kt, mask_t, states):
        probs, m_new, alpha = chain_softmax([src_ref[c] for c in range(len(chains))], mask_t, states)
        return chain_update(vst_ref[kt, g], probs, m_new, alpha, states)

    def pair(j, states):
        issue_scores(2 * j + 1, sb_ref)
        states = consume(sa_ref, 2 * j, None, states)
        issue_scores(2 * j + 2, sa_ref)
        return consume(sb_ref, 2 * j + 1, None, states)

    kd = t0 // TK
    mask_d = jnp.where(kd * TK - t0 + lax.broadcasted_iota(jnp.int32, (TK, 1), 0) <= dt, 0.0, NEG)
    init = tuple((jnp.full((1, CW), NEG, F32), jnp.zeros((V_ROWS, CW), F32)) for _ in chains)
    issue_scores(0, sa_ref)
    states = lax.fori_loop(0, kd // 2, pair, init)

    def odd_tail(states):
        issue_scores(kd, sb_ref)
        states = consume(sa_ref, kd - 1, None, states)
        return consume(sb_ref, kd, mask_d, states)

    states = lax.cond(kd % 2 == 1, odd_tail, lambda st: consume(sa_ref, kd, mask_d, st), states)
    o_s = normalise([acc for _, acc in states])

    sm_t = sm_ref[...].T
    mixed = []
    for r in range(R):
        gates = []
        for c in range(3):
            c0 = AG_OFF + r * 3 + c
            c1 = AG_OFF + (R + r) * 3 + c
            gates.append(jax.nn.sigmoid(jnp.where(g == 0, sm_t[c0:c0 + 1, :], sm_t[c1:c1 + 1, :])))
        mixed.append(gates[0] * o_c[:, lanes[r]] + gates[1] * o_s[:, lanes[r]] + gates[2] * o_w[:, lanes[r]])
    for r in range(0, R, 2):
        pair = jnp.concatenate([mixed[r], mixed[r + 1]], axis=0).T
        out_ref[:, r * D:(r + 2) * D] = pair.astype(out_ref.dtype)


def _nsa(zaq, sm, kc, vc, zkv, kconst, cconst, batch, seq, TQ, TK):
    n = batch * seq
    G = A_KV_GROUPS
    nq = seq // TQ
    ncp = kc.shape[1]
    per_batch = lambda a: pl.BlockSpec((None,) + a.shape[1:], lambda b, g, i: (b, 0, 0))
    kv_block = lambda j: pl.BlockSpec((seq, KV_WIDTH), functools.partial(lambda b, g, i, j: (b, j), j=j))
    return pl.pallas_call(
        functools.partial(_nsa_body, TQ=TQ, TK=TK, S=seq, NCP=ncp),
        grid=(batch, G, nq),
        in_specs=[
            pl.BlockSpec((TQ, A_REP * A_HEAD_DIM), lambda b, g, i: (b * nq + i, g)),
            pl.BlockSpec((TQ, NSM), lambda b, g, i: (b * nq + i, 0)),
            per_batch(kc), per_batch(vc),
            kv_block(0), kv_block(1), kv_block(2), kv_block(3),
            _resident(kconst.shape), _resident(cconst.shape),
        ],
        out_specs=pl.BlockSpec((TQ, A_REP * A_HEAD_DIM), lambda b, g, i: (b * nq + i, g)),
        out_shape=jax.ShapeDtypeStruct((n, A_WIDTH), BF16),
        scratch_shapes=[pltpu.VMEM((seq // TK, G, V_ROWS, TK), BF16), pltpu.VMEM((seq // TQ, G, V_ROWS, TQ), BF16),
                        pltpu.VMEM((G, A_HEAD_DIM, ncp), BF16)]
        + [pltpu.VMEM((A_REP * TQ // max(2 * LANES, TQ), TK, max(2 * LANES, TQ)), F32)] * 2,
        compiler_params=pltpu.CompilerParams(
            dimension_semantics=("arbitrary", "arbitrary", "arbitrary"), vmem_limit_bytes=VMEM_LIMIT),
    )(zaq, sm, kc, vc, zkv, zkv, zkv, zkv, kconst, cconst)


def _outblock_body(x_ref, mo_ref, ao_ref, p_ref, wout_ref, ln2_ref, wup_ref, cw_ref, cb_ref, wdn_ref,
                   pg_ref, wpg_ref, wpp_ref, fg_ref, out_ref, tail_ref, *, tm, tiles_per_seq, fc):
    @pl.when(pl.program_id(0) % tiles_per_seq == 0)
    def _():
        tail_ref[...] = jnp.zeros_like(tail_ref)

    x1 = (x_ref[...] + _dot(mo_ref[...], wout_ref[:M_WIDTH, :]) + _dot(ao_ref[...], wout_ref[M_WIDTH:, :]))
    ple = _dot(p_ref[...].astype(BF16), wpp_ref[...])
    h2 = _rms(x1, ln2_ref[...]).astype(BF16)
    rowi = lax.broadcasted_iota(jnp.int32, (tm, 1), 0)
    acc = jnp.zeros(x1.shape, F32)
    up = lambda c0: (_dot(h2, wup_ref[:, c0:c0 + fc]), _dot(h2, wup_ref[:, D_FF + c0:D_FF + c0 + fc]))
    nxt = up(0)
    for c0 in range(0, D_FF, fc):
        gate, val = nxt
        if c0 + fc < D_FF:
            nxt = up(c0 + fc)
        tail = tail_ref[:, c0:c0 + fc]
        g_m1 = jnp.where(rowi == 0, tail[7:8], pltpu.roll(gate, 1, 0))
        g_m2 = jnp.where(rowi == 0, tail[6:7], jnp.where(rowi == 1, tail[7:8], pltpu.roll(gate, 2, 0)))
        conv = (cw_ref[0:1, c0:c0 + fc] * g_m2 + cw_ref[1:2, c0:c0 + fc] * g_m1
                + cw_ref[2:3, c0:c0 + fc] * gate + cb_ref[:, c0:c0 + fc])
        y = jax.nn.silu(conv) * val
        acc = acc + _dot(y.astype(BF16), wdn_ref[c0:c0 + fc, :])
        tail_ref[:, c0:c0 + fc] = gate[tm - 8:tm]
    x2 = x1 + acc
    gate2 = jax.nn.sigmoid(_dot(_rms(x2, pg_ref[...]).astype(BF16), wpg_ref[...]))
    x3 = x2 + gate2 * ple
    out_ref[...] = _rms(x3, fg_ref[...])


def _outblock(x2d, mo, ao, p2d, wout, ln2, wup, cw, cb, wdn, pg, wpg, wpp, fg, seq, tm):
    n, d = x2d.shape
    row = lambda width: pl.BlockSpec((tm, width), lambda i: (i, 0))
    return pl.pallas_call(
        functools.partial(_outblock_body, tm=tm, tiles_per_seq=seq // tm, fc=512),
        grid=(n // tm,),
        in_specs=[
            row(d), row(M_WIDTH), row(A_WIDTH), row(p2d.shape[1]),
            _resident(wout.shape), _resident((1, d)), _resident(wup.shape), _resident(cw.shape),
            _resident((1, D_FF)), _resident(wdn.shape), _resident((1, d)), _resident(wpg.shape),
            _resident(wpp.shape), _resident((1, d)),
        ],
        out_specs=row(d),
        out_shape=jax.ShapeDtypeStruct((n, d), F32),
        scratch_shapes=[pltpu.VMEM((8, D_FF), F32)],
        compiler_params=pltpu.CompilerParams(
            dimension_semantics=("arbitrary",), vmem_limit_bytes=VMEM_LIMIT),
    )(x2d, mo, ao, p2d, wout, ln2, wup, cw, cb, wdn, pg, wpg, wpp, fg)


def _layer(x2d, p2d, batch, seq, ln1_g, w_in, gate_bias, mnorm_g, pe_k, pe_v, ck_w1, ck_w2, cv_w1, cv_w2,
           w_out, ln2_g, w_up, conv_w, conv_b, w_down, ple_g, w_pg, w_pp, out_g):
    d = x2d.shape[1]
    G = A_KV_GROUPS
    dh = A_HEAD_DIM
    o_mo = 3 * M_WIDTH
    o_mi = o_mo + M_WIDTH
    o_aq = o_mi + 2 * M_HEADS
    o_ag = o_aq + A_WIDTH + 6 * KV_WIDTH
    w = jnp.concatenate([
        w_in[:, :o_mo], w_in[:, o_aq:o_aq + A_WIDTH] * (dh ** -0.5 * LOG2E), w_in[:, o_aq + A_WIDTH:o_ag],
        w_in[:, o_mo:o_mi],
        w_in[:, o_mi:o_aq], w_in[:, o_ag:], jnp.zeros((d, NSM - 2 * M_HEADS - 3 * A_HEADS), w_in.dtype),
    ], axis=1).astype(BF16)
    bias = jnp.concatenate([gate_bias, jnp.zeros((NSM - 2 * M_HEADS,), F32)])[None, :]

    L = min(128, seq)
    zmq, zaq, zkc, zvc, zkv, zo, sm, smt = _inproj(x2d, ln1_g[None, :], w, bias, tm=512, chunk=L)
    nbt = max(n for n in (4, 2, 1) if batch % n == 0)
    m_out = _mlstm(zmq, zo, sm, smt, mnorm_g[None, :], batch, seq, L=L, NBT=nbt)

    TQ, TK = 256, min(512, seq)
    ncp = seq // CMP_STRIDE
    w1 = jnp.stack([ck_w1, cv_w1])
    eye = jnp.eye(G, dtype=F32)
    w1_blk = (w1.reshape(2, 2, CMP_STRIDE, 1, dh, 1, CMP_HIDDEN) * eye[None, None, None, :, None, :, None]
              ).reshape(2, 2, CMP_STRIDE * G * dh, G * CMP_HIDDEN).astype(BF16)
    w2 = jnp.stack([ck_w2, cv_w2])
    w2_blk = (w2[:, None, :, None, :] * eye[None, :, None, :, None]).reshape(
        2, G * CMP_HIDDEN, G * dh).astype(BF16)
    pe = jnp.stack([pe_k, pe_v]).reshape(2, 1, CMP_BLOCK * dh)
    pe = jnp.broadcast_to(pe, (2, 8, CMP_BLOCK * dh)).astype(BF16)
    row_view = lambda z: z.reshape(batch, ncp, CMP_STRIDE * KV_WIDTH)
    kc, vc = _compress(row_view(zkc), row_view(zvc), pe, w1.astype(BF16), w1_blk[:, 0], w1_blk[:, 1], w2_blk)

    pos = jnp.arange(seq)[:, None]
    onehot = pos // SEL_BLOCK == jnp.arange(LANES - dh)[None, :]
    pad = lambda n: jnp.zeros((n, LANES - dh - 6), jnp.int32)
    kconst = jnp.concatenate([onehot] + [pos // 256, pos % 256] * 3 + [pad(seq)], axis=1).astype(BF16)
    cpos = jnp.arange(ncp)[:, None] * CMP_STRIDE
    cconst = jnp.concatenate([jnp.zeros((ncp, LANES - dh), jnp.int32)] + [0 * cpos, cpos] * 3 + [pad(ncp)],
                             axis=1).astype(BF16)
    a_out = _nsa(zaq, sm, kc, vc, zkv, kconst, cconst, batch, seq, TQ=TQ, TK=TK)

    cw = jnp.concatenate([conv_w, jnp.zeros((8 - conv_w.shape[0], D_FF), F32)], axis=0)
    return _outblock(x2d, m_out, a_out, p2d, w_out.astype(BF16), ln2_g[None, :], w_up.astype(BF16), cw,
                     conv_b[None, :], w_down.astype(BF16), ple_g[None, :], w_pg.astype(BF16),
                     w_pp.astype(BF16), out_g[None, :], seq, tm=512)


def kernel(x, p, ln1_g, w_in, mlstm_gate_bias, mlstm_norm_g, cmp_pos_k, cmp_pos_v, cmp_k_w1, cmp_k_w2,
           cmp_v_w1, cmp_v_w2, w_out, ln2_g, w_up, conv_w, conv_b, w_down, ple_norm_g, w_ple_gate,
           w_ple_proj, final_g):
    batch, seq, d = x.shape
    depth = w_in.shape[0]
    assert depth == 1, "the fused output block applies the final norm, so a single layer is supported"
    assert seq % 128 == 0 and WINDOW + 128 <= seq <= SEL_BLOCK * (LANES - A_HEAD_DIM)
    x2d = x.reshape(batch * seq, d)
    i = 0
    out = _layer(x2d, p[i].reshape(batch * seq, -1), batch, seq, ln1_g[i], w_in[i], mlstm_gate_bias[i],
                 mlstm_norm_g[i], cmp_pos_k[i], cmp_pos_v[i], cmp_k_w1[i], cmp_k_w2[i], cmp_v_w1[i],
                 cmp_v_w2[i], w_out[i], ln2_g[i], w_up[i], conv_w[i], conv_b[i], w_down[i], ple_norm_g[i],
                 w_ple_gate[i], w_ple_proj[i], final_g)
    return out.reshape(batch, seq, d)
```

```python
import functools

import jax
import jax.numpy as jnp
from jax import lax
from jax.experimental import pallas as pl
from jax.experimental.pallas import tpu as pltpu

F32 = jnp.float32
BF16 = jnp.bfloat16

M_HEADS = 4
M_HEAD_DIM = 128
M_WIDTH = M_HEADS * M_HEAD_DIM
A_HEADS = 8
A_KV_GROUPS = 2
A_REP = A_HEADS // A_KV_GROUPS
A_HEAD_DIM = 64
A_WIDTH = A_HEADS * A_HEAD_DIM
KV_WIDTH = A_KV_GROUPS * A_HEAD_DIM
CMP_BLOCK = 32
CMP_STRIDE = 16
CMP_HIDDEN = 128
SEL_BLOCK = 64
N_SELECT = 16
WINDOW = 512
ALIBI_MAX = 8.0
D_FF = 2048
EPS = 1e-6
NEG = -1e30
BIG = 1e30
LOG2E = 1.4426950408889634

LANES = 128
VMEM_LIMIT = 56 * 1024 * 1024

NB = 3 * M_WIDTH + A_WIDTH + 6 * KV_WIDTH
NO = M_WIDTH
NSM = LANES
AQ_OFF = 3 * M_WIDTH
KV_OFF = AQ_OFF + A_WIDTH
AG_OFF = 2 * M_HEADS


def _dot(a, b):
    return jnp.dot(a, b, preferred_element_type=F32)


def _dot_nt(a, b):
    return lax.dot_general(a, b, (((1,), (1,)), ((), ())), preferred_element_type=F32)


def _rms(x, g):
    return x * lax.rsqrt(jnp.mean(x * x, axis=-1, keepdims=True) + EPS) * g


def _log_sigmoid(x):
    return jnp.minimum(x, 0.0) - jnp.log1p(jnp.exp(-jnp.abs(x)))


def _resident(shape):
    zeros = (0,) * len(shape)
    return pl.BlockSpec(shape, lambda *_: zeros, pipeline_mode=pl.Buffered(1))


def _inproj_body(x_ref, g_ref, w_ref, bias_ref, tri_ref, zm_ref, zq_ref, zkc_ref, zvc_ref, zkv_ref, zo_ref,
                 sm_ref, smt_ref):
    h = _rms(x_ref[...], g_ref[...]).astype(BF16)
    sm_t = (_dot(h, w_ref[:, NB + NO:]) + bias_ref[...]).T
    rows = 4 * M_HEADS
    gate = sm_t[:rows]
    row = lax.broadcasted_iota(jnp.int32, gate.shape, 0)
    is_f = (row >= M_HEADS) & (row < 2 * M_HEADS)
    rem = _log_sigmoid(gate)
    parts = []
    for _ in range(3):
        parts.append(rem.astype(BF16))
        rem = rem - parts[-1].astype(F32)

    cw = 512
    for ref, base in ((zm_ref, 0), (zq_ref, AQ_OFF), (zkc_ref, KV_OFF), (zvc_ref, KV_OFF + KV_WIDTH),
                      (zkv_ref, KV_OFF + 2 * KV_WIDTH)):
        for c0 in range(0, ref.shape[1], cw):
            c1 = min(c0 + cw, ref.shape[1])
            ref[:, c0:c1] = _dot(h, w_ref[:, base + c0:base + c1]).astype(BF16)
    zo_ref[...] = _dot(h, w_ref[:, NB:NB + NO])

    cum = _dot(parts[0], tri_ref[...]) + _dot(parts[1], tri_ref[...]) + _dot(parts[2], tri_ref[...])
    gate = jnp.where(is_f, cum, gate)
    smt_ref[...] = gate[:2 * M_HEADS]
    sm_ref[...] = jnp.concatenate([gate, sm_t[rows:]], axis=0).T


def _inproj(x2d, g, w, bias, tm, chunk):
    n, d = x2d.shape
    idx = jnp.arange(tm)
    tri = ((idx[:, None] // chunk == idx[None, :] // chunk) & (idx[:, None] <= idx[None, :])).astype(BF16)
    return pl.pallas_call(
        _inproj_body,
        grid=(n // tm,),
        in_specs=[
            pl.BlockSpec((tm, d), lambda i: (i, 0)),
            _resident((1, d)),
            _resident(w.shape),
            _resident((1, NSM)),
            _resident((tm, tm)),
        ],
        out_specs=[
            pl.BlockSpec((tm, AQ_OFF), lambda i: (i, 0)),
            pl.BlockSpec((tm, KV_OFF - AQ_OFF), lambda i: (i, 0)),
            pl.BlockSpec((tm, KV_WIDTH), lambda i: (i, 0)),
            pl.BlockSpec((tm, KV_WIDTH), lambda i: (i, 0)),
            pl.BlockSpec((tm, NB - KV_OFF - 2 * KV_WIDTH), lambda i: (i, 0)),
            pl.BlockSpec((tm, NO), lambda i: (i, 0)),
            pl.BlockSpec((tm, NSM), lambda i: (i, 0)),
            pl.BlockSpec((2 * M_HEADS, tm), lambda i: (0, i)),
        ],
        out_shape=[
            jax.ShapeDtypeStruct((n, AQ_OFF), BF16),
            jax.ShapeDtypeStruct((n, KV_OFF - AQ_OFF), BF16),
            jax.ShapeDtypeStruct((n, KV_WIDTH), BF16),
            jax.ShapeDtypeStruct((n, KV_WIDTH), BF16),
            jax.ShapeDtypeStruct((n, NB - KV_OFF - 2 * KV_WIDTH), BF16),
            jax.ShapeDtypeStruct((n, NO), F32),
            jax.ShapeDtypeStruct((n, NSM), F32),
            jax.ShapeDtypeStruct((2 * M_HEADS, n), F32),
        ],
        compiler_params=pltpu.CompilerParams(
            dimension_semantics=("arbitrary",), vmem_limit_bytes=VMEM_LIMIT),
    )(x2d, g, w, bias, tri)


def _mlstm_body(zb_ref, zo_ref, sm_ref, *rest, L, NBT):
    smt_refs, (gn_ref, out_ref, c_ref, m_ref) = rest[:NBT], rest[NBT:]

    @pl.when(pl.program_id(1) == 0)
    def _():
        c_ref[...] = jnp.zeros_like(c_ref)
        m_ref[...] = jnp.zeros_like(m_ref)

    dh = M_HEAD_DIM
    scale = dh ** -0.5
    causal = lax.broadcasted_iota(jnp.int32, (L, L), 1) <= lax.broadcasted_iota(jnp.int32, (L, L), 0)
    chains = [(bi, h) for bi in range(NBT) for h in range(M_HEADS)]
    each = lambda f: [f(i, bi, h) for i, (bi, h) in enumerate(chains)]
    col = lambda h: slice(h * dh, (h + 1) * dh)
    ones = (lax.broadcasted_iota(jnp.int32, (L, LANES), 1) == 0).astype(BF16)

    q = each(lambda i, bi, h: zb_ref[bi, :, col(h)])
    k = each(lambda i, bi, h: zb_ref[bi, :, M_WIDTH + h * dh:M_WIDTH + (h + 1) * dh])
    v = each(lambda i, bi, h: jnp.concatenate([zb_ref[bi, :, 2 * M_WIDTH + h * dh:2 * M_WIDTH + (h + 1) * dh],
                                               ones], axis=1))
    c_prev = each(lambda i, bi, h: c_ref[i])
    qk = each(lambda i, bi, h: _dot_nt(q[i], k[i]))
    qc = each(lambda i, bi, h: _dot(q[i], c_prev[i].astype(BF16)))

    ig_col = each(lambda i, bi, h: sm_ref[bi, :, h:h + 1])
    b_col = each(lambda i, bi, h: sm_ref[bi, :, M_HEADS + h:M_HEADS + h + 1])
    ig_row = each(lambda i, bi, h: smt_refs[bi][h:h + 1, :])
    b_row = each(lambda i, bi, h: smt_refs[bi][M_HEADS + h:M_HEADS + h + 1, :])
    b_last = each(lambda i, bi, h: b_row[i][:, L - 1:L])
    m_prev = each(lambda i, bi, h: m_ref[i][:, :1])

    dm = each(lambda i, bi, h: jnp.where(causal, b_col[i] - b_row[i] + ig_row[i], -jnp.inf))
    bq = each(lambda i, bi, h: b_col[i] + m_prev[i])
    dm_max = each(lambda i, bi, h: jnp.max(dm[i], axis=1, keepdims=True))
    m_loc = each(lambda i, bi, h: jnp.maximum(bq[i], dm_max[i]))
    inter = each(lambda i, bi, h: jnp.exp(bq[i] - m_loc[i]))
    wt = each(lambda i, bi, h: (jnp.exp(dm[i] - m_loc[i]) * qk[i]).astype(BF16))

    a_row = each(lambda i, bi, h: b_last[i] - b_row[i] + ig_row[i])
    a_max = each(lambda i, bi, h: jnp.max(a_row[i], axis=1, keepdims=True))
    m_new = each(lambda i, bi, h: jnp.maximum(b_last[i] + m_prev[i], a_max[i]))
    decay = each(lambda i, bi, h: jnp.exp(b_last[i] + m_prev[i] - m_new[i]))
    k_t = each(lambda i, bi, h: k[i].astype(F32).T)
    kw_t = each(lambda i, bi, h: (k_t[i] * jnp.exp(a_row[i] - m_new[i])).astype(BF16))

    wv = each(lambda i, bi, h: _dot(wt[i], v[i]))
    kv = each(lambda i, bi, h: _dot(kw_t[i], v[i]))
    for i in range(len(chains)):
        c_ref[i] = decay[i] * c_prev[i] + kv[i]
        m_ref[i] = jnp.broadcast_to(m_new[i], (1, LANES))

    tot = each(lambda i, bi, h: inter[i] * qc[i] + wv[i])
    floor = each(lambda i, bi, h: jnp.exp(-m_loc[i]))
    hh = each(lambda i, bi, h: (tot[i][:, :dh] * scale)
              / jnp.maximum(jnp.abs(tot[i][:, dh:dh + 1]) * scale, floor[i]))
    ms = each(lambda i, bi, h: jnp.mean(hh[i] * hh[i], axis=-1, keepdims=True))
    for i, (bi, h) in enumerate(chains):
        hn = hh[i] * lax.rsqrt(ms[i] + EPS) * gn_ref[:, col(h)]
        out_ref[bi, :, col(h)] = (hn * jax.nn.sigmoid(zo_ref[bi, :, col(h)])).astype(out_ref.dtype)


def _mlstm(zb, zo, sm, smt, gn, batch, seq, L, NBT):
    nc = seq // L
    rows = lambda width: pl.BlockSpec((NBT, L, width), lambda b, c: (b, c, 0))
    smt_specs = [pl.BlockSpec((2 * M_HEADS, L), functools.partial(lambda b, c, j: (0, (b * NBT + j) * nc + c), j=j))
                 for j in range(NBT)]
    nchain = NBT * M_HEADS
    out = pl.pallas_call(
        functools.partial(_mlstm_body, L=L, NBT=NBT),
        grid=(batch // NBT, nc),
        in_specs=[rows(zb.shape[-1]), rows(NO), rows(NSM)] + smt_specs + [_resident((1, M_WIDTH))],
        out_specs=rows(M_WIDTH),
        out_shape=jax.ShapeDtypeStruct((batch, seq, M_WIDTH), BF16),
        scratch_shapes=[
            pltpu.VMEM((nchain, M_HEAD_DIM, 2 * M_HEAD_DIM), F32),
            pltpu.VMEM((nchain, 1, LANES), F32),
        ],
        compiler_params=pltpu.CompilerParams(
            dimension_semantics=("arbitrary", "arbitrary"), vmem_limit_bytes=VMEM_LIMIT),
    )(zb.reshape(batch, seq, -1), zo.reshape(batch, seq, -1), sm.reshape(batch, seq, -1), *([smt] * NBT), gn)
    return out.reshape(batch * seq, M_WIDTH)


def _compress_body(xk_ref, xv_ref, pe_ref, w1_ref, wa_ref, wb_ref, w2_ref, kc_ref, vc_ref, *, ncp):
    for a, (x_ref, out_ref) in enumerate(((xk_ref, kc_ref), (xv_ref, vc_ref))):
        x = x_ref[...]
        first = _dot(x, wa_ref[a])
        second = _dot(x, wb_ref[a])
        second = pltpu.roll(second, ncp - 1, 0)
        pe_term = _dot(pe_ref[a], w1_ref[a])[0:1]
        hid = jax.nn.gelu(first + second + jnp.concatenate([pe_term] * A_KV_GROUPS, axis=1))
        out_ref[...] = _dot(hid.astype(BF16), w2_ref[a]).astype(out_ref.dtype)


def _compress(xk, xv, pe, w1, wa, wb, w2):
    batch, ncp, width = xk.shape
    x_spec = pl.BlockSpec((None, ncp, width), lambda b: (b, 0, 0))
    out_spec = pl.BlockSpec((None, ncp, KV_WIDTH), lambda b: (b, 0, 0))
    return pl.pallas_call(
        functools.partial(_compress_body, ncp=ncp),
        grid=(batch,),
        in_specs=[x_spec, x_spec] + [_resident(a.shape) for a in (pe, w1, wa, wb, w2)],
        out_specs=[out_spec, out_spec],
        out_shape=[jax.ShapeDtypeStruct((batch, ncp, KV_WIDTH), BF16)] * 2,
        compiler_params=pltpu.CompilerParams(
            dimension_semantics=("arbitrary",), vmem_limit_bytes=VMEM_LIMIT),
    )(xk, xv, pe, w1, wa, wb, w2)


V_ROWS = A_HEAD_DIM + 16


def _nsa_body(zq_ref, sm_ref, kc_ref, vc_ref, ks_ref, vs_ref, kw_ref, vw_ref, kconst_ref, cconst_ref, out_ref,
              vst_ref, vwt_ref, vct_ref, sa_ref, sb_ref, *, TQ, TK, S, NCP):
    g = pl.program_id(1)
    t0 = pl.program_id(2) * TQ
    R = A_REP
    M = R * TQ
    D = A_HEAD_DIM
    NS = S // SEL_BLOCK
    n_sel = min(N_SELECT, NS)
    lanes = [slice(r * TQ, (r + 1) * TQ) for r in range(R)]
    dt = lax.broadcasted_iota(jnp.int32, (1, TQ), 1)

    @pl.when((g == 0) & (pl.program_id(2) == 0))
    def _():
        vc_t = vc_ref[...].astype(F32).T.astype(BF16)
        ones_rows = (lax.broadcasted_iota(jnp.int32, (V_ROWS - D, 1), 0) == 0).astype(F32)
        for gi in range(A_KV_GROUPS):
            vct_ref[gi] = vc_t[gi * D:(gi + 1) * D]
        for v_ref, vt_ref in ((vs_ref, vst_ref), (vw_ref, vwt_ref)):
            t = vt_ref.shape[3]
            for j in range(S // t):
                v_t = v_ref[j * t:(j + 1) * t, :].astype(F32).T
                for gi in range(A_KV_GROUPS):
                    vt_ref[j, gi] = jnp.concatenate([v_t[gi * D:(gi + 1) * D],
                                                     jnp.broadcast_to(ones_rows, (V_ROWS - D, t))],
                                                    axis=0).astype(BF16)

    slope = jnp.zeros((1, M), F32)
    head = lax.broadcasted_iota(jnp.int32, (1, M), 1) >> (TQ.bit_length() - 1)
    for r in range(R):
        s_r = jnp.where(g == 0, 2.0 ** (-ALIBI_MAX * (r + 1) / A_HEADS),
                        2.0 ** (-ALIBI_MAX * (R + r + 1) / A_HEADS)).astype(F32)
        slope = jnp.where(head == r, s_r, slope)
    row8 = lax.broadcasted_iota(jnp.int32, (8, M), 0)
    rem = slope * LOG2E
    feat = jnp.zeros((8, M), F32)
    for j in range(3):
        term = rem.astype(BF16).astype(F32)
        feat = jnp.where(row8 == 2 * j, term * 256.0, jnp.where(row8 == 2 * j + 1, term, feat))
        rem = rem - term
    q_t = zq_ref[...].astype(F32).T
    q_f = jnp.concatenate([q_t[r * D:(r + 1) * D] for r in range(R)], axis=1)
    zeros = lambda n: jnp.zeros((n, M), F32)
    q_grp = jnp.where(g == 0, jnp.concatenate([q_f, zeros(D)], axis=0),
                      jnp.concatenate([zeros(D), q_f], axis=0))

    CW = max(2 * LANES, TQ)
    chains = [slice(c * CW, (c + 1) * CW) for c in range(M // CW)]

    cat = lambda xs: jnp.concatenate(xs, axis=1)
    hpc = CW // TQ

    def chain_scores(k_tile, q_aug):
        return [_dot(k_tile, q_aug[:, ch]) for ch in chains]

    def chain_softmax(scores, mask_t, states):
        blks = [s_t[:, lanes[r]] for s_t in scores for r in range(hpc)]
        if mask_t is not None:
            blks = [blk + mask_t for blk in blks]
        m_new = [jnp.max(blk, axis=0, keepdims=True) for blk in blks]
        alpha = None
        if states is not None:
            m_old = [st[0][:, lanes[r]] for st in states for r in range(hpc)]
            m_new = [jnp.maximum(a, b) for a, b in zip(m_new, m_old)]
            alpha = [jnp.exp2(a - b) for a, b in zip(m_old, m_new)]
        probs = [jnp.exp2((blk - m).astype(BF16)) for blk, m in zip(blks, m_new)]
        group = lambda xs: [cat(xs[c * hpc:(c + 1) * hpc]) for c in range(len(chains))]
        return group(probs), group(m_new), (group(alpha) if alpha else None)

    def chain_update(v_tile, probs, m_new, alpha, states):
        pvs = [_dot(v_tile, p) for p in probs]
        return tuple((m, a * st[1] + pv) for m, a, st, pv in zip(m_new, alpha, states, pvs))

    def normalise(accs):
        acc = jnp.concatenate(accs, axis=1)
        return acc[:D] * (1.0 / acc[D:D + 1])

    q_plain = jnp.concatenate([q_grp, zeros(LANES - D), feat, zeros(LANES - D - 8)], axis=0).astype(BF16)
    wk = min(WINDOW + TQ, S)
    k0 = pl.multiple_of(jnp.maximum(t0 + TQ - wk, 0), TQ)
    k_band = jnp.concatenate([kw_ref[pl.ds(k0, wk), :], kconst_ref[pl.ds(k0, wk), :]], axis=1)
    v_band = vwt_ref[pl.ds(k0 // TQ, wk // TQ), g]
    v_band = jnp.concatenate([v_band[j] for j in range(wk // TQ)], axis=1)
    dist = dt - (k0 - t0 + lax.broadcasted_iota(jnp.int32, (wk, 1), 0))
    mask_w = jnp.where((dist >= 0) & (dist < WINDOW), 0.0, NEG)
    cend = lax.broadcasted_iota(jnp.int32, (NCP, 1), 0) * CMP_STRIDE + (CMP_BLOCK - 1) - t0
    mask_c = jnp.where(cend <= dt, 0.0, NEG)

    s_w = chain_scores(k_band, q_plain)
    s_c = _dot(jnp.concatenate([kc_ref[...], cconst_ref[...]], axis=1), q_plain)
    p_w, _, _ = chain_softmax(s_w, mask_w, None)
    p_sum = jnp.zeros((NCP, TQ), F32)
    p_c = []
    for r in range(R):
        blk = s_c[:, lanes[r]] + mask_c
        mx = jnp.maximum(jnp.max(blk, axis=0, keepdims=True), 0.1 * NEG)
        e = jnp.exp2(blk - mx)
        p = e * (1.0 / jnp.maximum(jnp.sum(e, axis=0, keepdims=True), 1e-30))
        p_sum = p_sum + p
        p_c.append(p.astype(BF16))
    o_w = normalise([_dot(v_band, p) for p in p_w])
    o_c = _dot(vct_ref[g], cat(p_c))

    jc = lax.broadcasted_iota(jnp.int32, (NS, 1), 0)
    nr = lax.broadcasted_iota(jnp.int32, (1, NCP), 1)
    overlap_t = ((nr * CMP_STRIDE <= jc * SEL_BLOCK + (SEL_BLOCK - 1))
                 & (nr * CMP_STRIDE + (CMP_BLOCK - 1) >= jc * SEL_BLOCK))
    imp_t = _dot(jnp.where(overlap_t, 1.0, 0.0).astype(BF16), p_sum.astype(BF16))
    cur = (t0 + dt) >> (SEL_BLOCK.bit_length() - 1)
    forced = (jc == 0) | (jc == cur) | (jc == cur - 1)
    score = jnp.where(jc <= cur, jnp.where(forced, BIG, imp_t), NEG)
    sub = lax.broadcasted_iota(jnp.int32, (8, TQ), 0)
    blocks = [score[8 * jb:8 * jb + 8] for jb in range(NS // 8)]
    ranks = [jnp.zeros((8, TQ), F32) for _ in blocks]
    for i in range(NS):
        s_i = score[i:i + 1, :]
        for jb, blk in enumerate(blocks):
            if 8 * jb > i:
                beats = jnp.where(s_i >= blk, 1.0, 0.0)
            elif 8 * jb + 7 < i:
                beats = jnp.where(s_i > blk, 1.0, 0.0)
            else:
                beats = jnp.where(sub + 8 * jb > i, jnp.where(s_i >= blk, 1.0, 0.0), jnp.where(s_i > blk, 1.0, 0.0))
            ranks[jb] = ranks[jb] + beats
    sel_t = jnp.where(jnp.concatenate(ranks, axis=0) < n_sel, 0.0, NEG)
    sel_rows = [jnp.concatenate([sel_t] * R, axis=1)]
    if NS < LANES - D:
        sel_rows.append(zeros(LANES - D - NS))
    q_sel = jnp.concatenate([q_grp] + sel_rows + [feat, zeros(LANES - D - 8)], axis=0).astype(BF16)

    def issue_scores(kt, dst_ref):
        rows = pl.ds(pl.multiple_of(kt * TK, TK), TK)
        k_tile = jnp.concatenate([ks_ref[rows, :], kconst_ref[rows, :]], axis=1)
        for c, s_t in enumerate(chain_scores(k_tile, q_sel)):
            dst_ref[c] = s_t

    def consume(src_ref, kt, mask_t, states):
        probs, m_new, alpha = chain_softmax([src_ref[c] for c in range(len(chains))], mask_t, states)
        return chain_update(vst_ref[kt, g], probs, m_new, alpha, states)

    def pair(j, states):
        issue_scores(2 * j + 1, sb_ref)
        states = consume(sa_ref, 2 * j, None, states)
        issue_scores(2 * j + 2, sa_ref)
        return consume(sb_ref, 2 * j + 1, None, states)

    kd = t0 // TK
    mask_d = jnp.where(kd * TK - t0 + lax.broadcasted_iota(jnp.int32, (TK, 1), 0) <= dt, 0.0, NEG)
    init = tuple((jnp.full((1, CW), NEG, F32), jnp.zeros((V_ROWS, CW), F32)) for _ in chains)
    issue_scores(0, sa_ref)
    states = lax.fori_loop(0, kd // 2, pair, init)

    def odd_tail(states):
        issue_scores(kd, sb_ref)
        states = consume(sa_ref, kd - 1, None, states)
        return consume(sb_ref, kd, mask_d, states)

    states = lax.cond(kd % 2 == 1, odd_tail, lambda st: consume(sa_ref, kd, mask_d, st), states)
    o_s = normalise([acc for _, acc in states])

    sm_t = sm_ref[...].T
    mixed = []
    for r in range(R):
        gates = []
        for c in range(3):
            c0 = AG_OFF + r * 3 + c
            c1 = AG_OFF + (R + r) * 3 + c
            gates.append(jax.nn.sigmoid(jnp.where(g == 0, sm_t[c0:c0 + 1, :], sm_t[c1:c1 + 1, :])))
        mixed.append(gates[0] * o_c[:, lanes[r]] + gates[1] * o_s[:, lanes[r]] + gates[2] * o_w[:, lanes[r]])
    for r in range(0, R, 2):
        pair = jnp.concatenate([mixed[r], mixed[r + 1]], axis=0).T
        out_ref[:, r * D:(r + 2) * D] = pair.astype(out_ref.dtype)


def _nsa(zaq, sm, kc, vc, zkv, kconst, cconst, batch, seq, TQ, TK):
    n = batch * seq
    G = A_KV_GROUPS
    nq = seq // TQ
    ncp = kc.shape[1]
    per_batch = lambda a: pl.BlockSpec((None,) + a.shape[1:], lambda b, g, i: (b, 0, 0))
    kv_block = lambda j: pl.BlockSpec((seq, KV_WIDTH), functools.partial(lambda b, g, i, j: (b, j), j=j))
    return pl.pallas_call(
        functools.partial(_nsa_body, TQ=TQ, TK=TK, S=seq, NCP=ncp),
        grid=(batch, G, nq),
        in_specs=[
            pl.BlockSpec((TQ, A_REP * A_HEAD_DIM), lambda b, g, i: (b * nq + i, g)),
            pl.BlockSpec((TQ, NSM), lambda b, g, i: (b * nq + i, 0)),
            per_batch(kc), per_batch(vc),
            kv_block(0), kv_block(1), kv_block(2), kv_block(3),
            _resident(kconst.shape), _resident(cconst.shape),
        ],
        out_specs=pl.BlockSpec((TQ, A_REP * A_HEAD_DIM), lambda b, g, i: (b * nq + i, g)),
        out_shape=jax.ShapeDtypeStruct((n, A_WIDTH), BF16),
        scratch_shapes=[pltpu.VMEM((seq // TK, G, V_ROWS, TK), BF16), pltpu.VMEM((seq // TQ, G, V_ROWS, TQ), BF16),
                        pltpu.VMEM((G, A_HEAD_DIM, ncp), BF16)]
        + [pltpu.VMEM((A_REP * TQ // max(2 * LANES, TQ), TK, max(2 * LANES, TQ)), F32)] * 2,
        compiler_params=pltpu.CompilerParams(
            dimension_semantics=("arbitrary", "arbitrary", "arbitrary"), vmem_limit_bytes=VMEM_LIMIT),
    )(zaq, sm, kc, vc, zkv, zkv, zkv, zkv, kconst, cconst)


def _outblock_body(x_ref, mo_ref, ao_ref, p_ref, wout_ref, ln2_ref, wup_ref, cw_ref, cb_ref, wdn_ref,
                   pg_ref, wpg_ref, wpp_ref, fg_ref, out_ref, tail_ref, *, tm, tiles_per_seq, fc):
    @pl.when(pl.program_id(0) % tiles_per_seq == 0)
    def _():
        tail_ref[...] = jnp.zeros_like(tail_ref)

    x1 = (x_ref[...] + _dot(mo_ref[...], wout_ref[:M_WIDTH, :]) + _dot(ao_ref[...], wout_ref[M_WIDTH:, :]))
    ple = _dot(p_ref[...].astype(BF16), wpp_ref[...])
    h2 = _rms(x1, ln2_ref[...]).astype(BF16)
    rowi = lax.broadcasted_iota(jnp.int32, (tm, 1), 0)
    acc = jnp.zeros(x1.shape, F32)
    up = lambda c0: (_dot(h2, wup_ref[:, c0:c0 + fc]), _dot(h2, wup_ref[:, D_FF + c0:D_FF + c0 + fc]))
    nxt = up(0)
    for c0 in range(0, D_FF, fc):
        gate, val = nxt
        if c0 + fc < D_FF:
            nxt = up(c0 + fc)
        tail = tail_ref[:, c0:c0 + fc]
        g_m1 = jnp.where(rowi == 0, tail[7:8], pltpu.roll(gate, 1, 0))
        g_m2 = jnp.where(rowi == 0, tail[6:7], jnp.where(rowi == 1, tail[7:8], pltpu.roll(gate, 2, 0)))
        conv = (cw_ref[0:1, c0:c0 + fc] * g_m2 + cw_ref[1:2, c0:c0 + fc] * g_m1
                + cw_ref[2:3, c0:c0 + fc] * gate + cb_ref[:, c0:c0 + fc])
        y = jax.nn.silu(conv) * val
        acc = acc + _dot(y.astype(BF16), wdn_ref[c0:c0 + fc, :])
        tail_ref[:, c0:c0 + fc] = gate[tm - 8:tm]
    x2 = x1 + acc
    gate2 = jax.nn.sigmoid(_dot(_rms(x2, pg_ref[...]).astype(BF16), wpg_ref[...]))
    x3 = x2 + gate2 * ple
    out_ref[...] = _rms(x3, fg_ref[...])


def _outblock(x2d, mo, ao, p2d, wout, ln2, wup, cw, cb, wdn, pg, wpg, wpp, fg, seq, tm):
    n, d = x2d.shape
    row = lambda width: pl.BlockSpec((tm, width), lambda i: (i, 0))
    return pl.pallas_call(
        functools.partial(_outblock_body, tm=tm, tiles_per_seq=seq // tm, fc=512),
        grid=(n // tm,),
        in_specs=[
            row(d), row(M_WIDTH), row(A_WIDTH), row(p2d.shape[1]),
            _resident(wout.shape), _resident((1, d)), _resident(wup.shape), _resident(cw.shape),
            _resident((1, D_FF)), _resident(wdn.shape), _resident((1, d)), _resident(wpg.shape),
            _resident(wpp.shape), _resident((1, d)),
        ],
        out_specs=row(d),
        out_shape=jax.ShapeDtypeStruct((n, d), F32),
        scratch_shapes=[pltpu.VMEM((8, D_FF), F32)],
        compiler_params=pltpu.CompilerParams(
            dimension_semantics=("arbitrary",), vmem_limit_bytes=VMEM_LIMIT),
    )(x2d, mo, ao, p2d, wout, ln2, wup, cw, cb, wdn, pg, wpg, wpp, fg)


def _layer(x2d, p2d, batch, seq, ln1_g, w_in, gate_bias, mnorm_g, pe_k, pe_v, ck_w1, ck_w2, cv_w1, cv_w2,
           w_out, ln2_g, w_up, conv_w, conv_b, w_down, ple_g, w_pg, w_pp, out_g):
    d = x2d.shape[1]
    G = A_KV_GROUPS
    dh = A_HEAD_DIM
    o_mo = 3 * M_WIDTH
    o_mi = o_mo + M_WIDTH
    o_aq = o_mi + 2 * M_HEADS
    o_ag = o_aq + A_WIDTH + 6 * KV_WIDTH
    w = jnp.concatenate([
        w_in[:, :o_mo], w_in[:, o_aq:o_aq + A_WIDTH] * (dh ** -0.5 * LOG2E), w_in[:, o_aq + A_WIDTH:o_ag],
        w_in[:, o_mo:o_mi],
        w_in[:, o_mi:o_aq], w_in[:, o_ag:], jnp.zeros((d, NSM - 2 * M_HEADS - 3 * A_HEADS), w_in.dtype),
    ], axis=1).astype(BF16)
    bias = jnp.concatenate([gate_bias, jnp.zeros((NSM - 2 * M_HEADS,), F32)])[None, :]

    L = min(128, seq)
    zmq, zaq, zkc, zvc, zkv, zo, sm, smt = _inproj(x2d, ln1_g[None, :], w, bias, tm=512, chunk=L)
    nbt = max(n for n in (4, 2, 1) if batch % n == 0)
    m_out = _mlstm(zmq, zo, sm, smt, mnorm_g[None, :], batch, seq, L=L, NBT=nbt)

    TQ, TK = 256, min(256, seq)
    ncp = seq // CMP_STRIDE
    w1 = jnp.stack([ck_w1, cv_w1])
    eye = jnp.eye(G, dtype=F32)
    w1_blk = (w1.reshape(2, 2, CMP_STRIDE, 1, dh, 1, CMP_HIDDEN) * eye[None, None, None, :, None, :, None]
              ).reshape(2, 2, CMP_STRIDE * G * dh, G * CMP_HIDDEN).astype(BF16)
    w2 = jnp.stack([ck_w2, cv_w2])
    w2_blk = (w2[:, None, :, None, :] * eye[None, :, None, :, None]).reshape(
        2, G * CMP_HIDDEN, G * dh).astype(BF16)
    pe = jnp.stack([pe_k, pe_v]).reshape(2, 1, CMP_BLOCK * dh)
    pe = jnp.broadcast_to(pe, (2, 8, CMP_BLOCK * dh)).astype(BF16)
    row_view = lambda z: z.reshape(batch, ncp, CMP_STRIDE * KV_WIDTH)
    kc, vc = _compress(row_view(zkc), row_view(zvc), pe, w1.astype(BF16), w1_blk[:, 0], w1_blk[:, 1], w2_blk)

    pos = jnp.arange(seq)[:, None]
    onehot = pos // SEL_BLOCK == jnp.arange(LANES - dh)[None, :]
    pad = lambda n: jnp.zeros((n, LANES - dh - 6), jnp.int32)
    kconst = jnp.concatenate([onehot] + [pos // 256, pos % 256] * 3 + [pad(seq)], axis=1).astype(BF16)
    cpos = jnp.arange(ncp)[:, None] * CMP_STRIDE
    cconst = jnp.concatenate([jnp.zeros((ncp, LANES - dh), jnp.int32)] + [0 * cpos, cpos] * 3 + [pad(ncp)],
                             axis=1).astype(BF16)
    a_out = _nsa(zaq, sm, kc, vc, zkv, kconst, cconst, batch, seq, TQ=TQ, TK=TK)

    cw = jnp.concatenate([conv_w, jnp.zeros((8 - conv_w.shape[0], D_FF), F32)], axis=0)
    return _outblock(x2d, m_out, a_out, p2d, w_out.astype(BF16), ln2_g[None, :], w_up.astype(BF16), cw,
                     conv_b[None, :], w_down.astype(BF16), ple_g[None, :], w_pg.astype(BF16),
                     w_pp.astype(BF16), out_g[None, :], seq, tm=512)


def kernel(x, p, ln1_g, w_in, mlstm_gate_bias, mlstm_norm_g, cmp_pos_k, cmp_pos_v, cmp_k_w1, cmp_k_w2,
           cmp_v_w1, cmp_v_w2, w_out, ln2_g, w_up, conv_w, conv_b, w_down, ple_norm_g, w_ple_gate,
           w_ple_proj, final_g):
    batch, seq, d = x.shape
    depth = w_in.shape[0]
    assert depth == 1, "the fused output block applies the final norm, so a single layer is supported"
    assert seq % 128 == 0 and WINDOW + 128 <= seq <= SEL_BLOCK * (LANES - A_HEAD_DIM)
    x2d = x.reshape(batch * seq, d)
    i = 0
    out = _layer(x2d, p[i].reshape(batch * seq, -1), batch, seq, ln1_g[i], w_in[i], mlstm_gate_bias[i],
                 mlstm_norm_g[i], cmp_pos_k[i], cmp_pos_v[i], cmp_k_w1[i], cmp_k_w2[i], cmp_v_w1[i],
                 cmp_v_w2[i], w_out[i], ln2_g[i], w_up[i], conv_w[i], conv_b[i], w_down[i], ple_norm_g[i],
                 w_ple_gate[i], w_ple_proj[i], final_g)
    return out.reshape(batch, seq, d)
```
